```python
import jax, jax.numpy as jnp
from jax import lax
import numpy as np

D_MODEL = 1024
BATCH = 4
SEQ = 4096
DEPTH = 2
DEC_BATCH = 128
DEC_SEQ = 1
PAST_LEN = 16384
PAGE_SIZE = 128

CONV_WIDTH = D_MODEL
CONV_KERNEL = 31
CONV_BUF = CONV_KERNEL - 1
N_HEADS = 16
HEAD_DIM = 64
N_KV_HEADS = 2
GROUP = N_HEADS // N_KV_HEADS
ATTN_WIDTH = N_HEADS * HEAD_DIM
KV_WIDTH = N_KV_HEADS * HEAD_DIM
WINDOW = 128
BLOCK = 128
ROPE_DIM = HEAD_DIM // 4
ROPE_THETA = 500000.0
EPS = 1e-6
NEG = -1e30
IN_SIZES = (2 * CONV_WIDTH, CONV_WIDTH, ATTN_WIDTH, KV_WIDTH, KV_WIDTH, ATTN_WIDTH, D_MODEL, D_MODEL)
IN_COLS = sum(IN_SIZES)

kernel_name = "hybrid_conformer_swa_sink_gated_step"


def rms_norm(x, g):
    xf = x.astype(jnp.float32)
    y = xf * lax.rsqrt(jnp.mean(xf * xf, axis=-1, keepdims=True) + EPS)
    return (y * g.astype(jnp.float32)).astype(x.dtype)


def layer_norm(x, g, b):
    xf = x.astype(jnp.float32)
    mu = jnp.mean(xf, axis=-1, keepdims=True)
    var = jnp.mean(jnp.square(xf - mu), axis=-1, keepdims=True)
    y = (xf - mu) * lax.rsqrt(var + EPS)
    return (y * g.astype(jnp.float32) + b.astype(jnp.float32)).astype(x.dtype)


def partial_rope(x, pos):
    half = ROPE_DIM // 2
    inv = ROPE_THETA ** (-jnp.arange(0, ROPE_DIM, 2, dtype=jnp.float32) / ROPE_DIM)
    ang = pos.astype(jnp.float32)[:, None] * inv[None, :]
    cos = jnp.cos(ang)[None, :, None, :]
    sin = jnp.sin(ang)[None, :, None, :]
    xf = x.astype(jnp.float32)
    x1, x2, rest = xf[..., :half], xf[..., half:ROPE_DIM], xf[..., ROPE_DIM:]
    out = jnp.concatenate([x1 * cos - x2 * sin, x2 * cos + x1 * sin, rest], axis=-1)
    return out.astype(x.dtype)


def split_in(z):
    idx = np.cumsum(np.array(IN_SIZES))[:-1].tolist()
    return jnp.split(z, idx, axis=-1)


def sink_attention(q, k, v, mask, sinks):
    s = jnp.einsum('nbqkgd,nbskd->nbkgqs', q.astype(jnp.float32), k.astype(jnp.float32))
    s = s * (HEAD_DIM ** -0.5)
    s = jnp.where(mask[None, :, None, None], s, NEG)
    sk = sinks.astype(jnp.float32).reshape(1, 1, N_KV_HEADS, GROUP, 1, 1)
    m = jnp.maximum(jnp.max(s, axis=-1, keepdims=True), sk)
    p = jnp.exp(s - m)
    denom = jnp.sum(p, axis=-1, keepdims=True) + jnp.exp(sk - m)
    o = jnp.einsum('nbkgqs,nbskd->nbqkgd', p / denom, v.astype(jnp.float32))
    return o.astype(q.dtype)


def attend_prompt(q, k, v, sinks):
    n, t = q.shape[0], q.shape[1]
    nb = t // BLOCK
    qb = q.reshape(n, nb, BLOCK, N_KV_HEADS, GROUP, HEAD_DIM)
    kb = k.reshape(n, nb, BLOCK, N_KV_HEADS, HEAD_DIM)
    vb = v.reshape(n, nb, BLOCK, N_KV_HEADS, HEAD_DIM)
    zero = jnp.zeros_like(kb[:, :1])
    kk = jnp.concatenate([jnp.concatenate([zero, kb[:, :-1]], axis=1), kb], axis=2)
    vv = jnp.concatenate([jnp.concatenate([zero, vb[:, :-1]], axis=1), vb], axis=2)
    i = jnp.arange(BLOCK)[None, :, None]
    j = jnp.arange(2 * BLOCK)[None, None, :]
    blk = jnp.arange(nb)[:, None, None]
    diff = BLOCK + i - j
    kpos = (blk - 1) * BLOCK + j
    mask = (diff >= 0) & (diff < WINDOW) & (kpos >= 0)
    o = sink_attention(qb, kk, vv, mask, sinks)
    return o.reshape(n, t, ATTN_WIDTH), k[:, -WINDOW:], v[:, -WINDOW:]


def make_attend_sample(k_buf, v_buf):
    def attend_sample(q, k, v, sinks):
        n, t = q.shape[0], q.shape[1]
        kk = jnp.concatenate([k_buf, k], axis=1)
        vv = jnp.concatenate([v_buf, v], axis=1)
        qpos = PAST_LEN + jnp.arange(t)
        kpos = jnp.concatenate([PAST_LEN - WINDOW + jnp.arange(WINDOW), qpos])
        diff = qpos[:, None] - kpos[None, :]
        mask = ((diff >= 0) & (diff < WINDOW))[None]
        qb = q.reshape(n, 1, t, N_KV_HEADS, GROUP, HEAD_DIM)
        o = sink_attention(qb, kk[:, None], vv[:, None], mask, sinks)
        return o.reshape(n, t, ATTN_WIDTH), kk[:, -WINDOW:], vv[:, -WINDOW:]
    return attend_sample


def hybrid_layer(x, pos, conv_buf, attend, norm_g, w_in, conv_w, conv_b, ln_g, ln_b,
                 w_conv_out, sinks, w_attn_out, w_out):
    n, t, _ = x.shape
    h = rms_norm(x, norm_g)
    z = jnp.einsum('ntd,dc->ntc', h, w_in)
    glu, gate_a, q, k, v, gate_b, mg_a, mg_b = split_in(z)
    u = glu[..., :CONV_WIDTH] * jax.nn.sigmoid(glu[..., CONV_WIDTH:])
    full = jnp.concatenate([conv_buf, u], axis=1)
    c = lax.conv_general_dilated(full, conv_w[:, None, :], window_strides=(1,), padding='VALID',
                                 dimension_numbers=('NWC', 'WIO', 'NWC'),
                                 feature_group_count=CONV_WIDTH) + conv_b
    c = jax.nn.silu(layer_norm(c, ln_g, ln_b)) * jax.nn.silu(gate_a)
    y_a = jnp.einsum('ntc,cd->ntd', c, w_conv_out)
    new_conv = full[:, -CONV_BUF:]
    q = partial_rope(q.reshape(n, t, N_HEADS, HEAD_DIM), pos)
    k = partial_rope(k.reshape(n, t, N_KV_HEADS, HEAD_DIM), pos)
    v = v.reshape(n, t, N_KV_HEADS, HEAD_DIM)
    o, new_k, new_v = attend(q, k, v, sinks)
    y_b = jnp.einsum('nte,ed->ntd', o * jax.nn.silu(gate_b), w_attn_out)
    y = jax.nn.sigmoid(mg_a) * y_a + jax.nn.sigmoid(mg_b) * y_b
    return x + jnp.einsum('ntd,de->nte', y, w_out), new_conv, new_k, new_v


def setup_inputs(seed: int = 0) -> dict:
    key = jax.random.key(seed)
    ks = jax.random.split(key, 17)
    f = jnp.float32
    nrm = lambda k, shape, s: jax.random.normal(k, shape, f) * s
    return {
        "x_prompt": nrm(ks[0], (BATCH, SEQ, D_MODEL), 1.0),
        "x_sample": nrm(ks[1], (DEC_BATCH, DEC_SEQ, D_MODEL), 1.0),
        "state_conv": nrm(ks[2], (DEPTH, DEC_BATCH, CONV_BUF, CONV_WIDTH), 0.5),
        "cache_k_win": nrm(ks[3], (DEPTH, DEC_BATCH, WINDOW, N_KV_HEADS, HEAD_DIM), 1.0),
        "cache_v_win": nrm(ks[4], (DEPTH, DEC_BATCH, WINDOW, N_KV_HEADS, HEAD_DIM), 1.0),
        "norm_g": 1.0 + nrm(ks[5], (DEPTH, D_MODEL), 0.1),
        "w_in": nrm(ks[6], (DEPTH, D_MODEL, IN_COLS), D_MODEL ** -0.5),
        "conv_w": nrm(ks[7], (DEPTH, CONV_KERNEL, CONV_WIDTH), CONV_KERNEL ** -0.5),
        "conv_b": nrm(ks[8], (DEPTH, CONV_WIDTH), 0.02),
        "conv_ln_g": 1.0 + nrm(ks[9], (DEPTH, CONV_WIDTH), 0.1),
        "conv_ln_b": nrm(ks[10], (DEPTH, CONV_WIDTH), 0.02),
        "w_conv_out": nrm(ks[11], (DEPTH, CONV_WIDTH, D_MODEL), CONV_WIDTH ** -0.5),
        "attn_sinks": nrm(ks[12], (DEPTH, N_HEADS), 0.5),
        "w_attn_out": nrm(ks[13], (DEPTH, ATTN_WIDTH, D_MODEL), ATTN_WIDTH ** -0.5),
        "w_out": nrm(ks[14], (DEPTH, D_MODEL, D_MODEL), D_MODEL ** -0.5),
        "final_norm_g": 1.0 + nrm(ks[15], (D_MODEL,), 0.1),
    }


def reference(x_prompt, x_sample, state_conv, cache_k_win, cache_v_win, norm_g, w_in, conv_w,
              conv_b, conv_ln_g, conv_ln_b, w_conv_out, attn_sinks, w_attn_out, w_out, final_norm_g):
    t_p = x_prompt.shape[1]
    t_s = x_sample.shape[1]
    pos_p = jnp.arange(t_p)
    pos_s = PAST_LEN + jnp.arange(t_s)
    hp, hs = x_prompt, x_sample
    conv_p, k_p, v_p, conv_s, k_s, v_s = [], [], [], [], [], []
    for l in range(DEPTH):
        params = (norm_g[l], w_in[l], conv_w[l], conv_b[l], conv_ln_g[l], conv_ln_b[l],
                  w_conv_out[l], attn_sinks[l], w_attn_out[l], w_out[l])
        zero_buf = jnp.zeros((hp.shape[0], CONV_BUF, CONV_WIDTH), hp.dtype)
        hp, c1, k1, v1 = hybrid_layer(hp, pos_p, zero_buf, attend_prompt, *params)
        hs, c2, k2, v2 = hybrid_layer(hs, pos_s, state_conv[l],
                                      make_attend_sample(cache_k_win[l], cache_v_win[l]), *params)
        conv_p.append(c1); k_p.append(k1); v_p.append(v1)
        conv_s.append(c2); k_s.append(k2); v_s.append(v2)
    y_prompt = rms_norm(hp, final_norm_g)
    y_sample = rms_norm(hs, final_norm_g)
    new_conv_prompt = jnp.stack(conv_p)
    new_k_prompt = jnp.stack(k_p)
    new_v_prompt = jnp.stack(v_p)
    new_conv_sample = jnp.stack(conv_s)
    new_k_sample = jnp.stack(k_s)
    new_v_sample = jnp.stack(v_s)
    return (y_prompt, y_sample, new_conv_prompt, new_k_prompt, new_v_prompt,
            new_conv_sample, new_k_sample, new_v_sample)
```

```python
import functools
import math

import jax
import jax.numpy as jnp
from jax import lax
from jax.experimental import pallas as pl
from jax.experimental.pallas import tpu as pltpu

D_MODEL = 1024
CONV_KERNEL = 31
CONV_BUF = CONV_KERNEL - 1
N_HEADS = 16
HEAD_DIM = 64
N_KV_HEADS = 2
GROUP = N_HEADS // N_KV_HEADS
KV_WIDTH = N_KV_HEADS * HEAD_DIM
WINDOW = 128
BLOCK = 128
ROPE_DIM = HEAD_DIM // 4
ROPE_THETA = 500000.0
EPS = 1e-6
NEG = -1e30
PAST_LEN = 16384

COL_GLU_A = 0
COL_GLU_B = 1024
COL_GATE_A = 2048
COL_Q = 3072
COL_K = 4096
COL_V = 4224
COL_GATE_B = 4352
COL_MG_A = 5376
COL_MG_B = 6400
IN_COLS = 7424

LANES = 128
N_TILES = D_MODEL // LANES
CONV_PAD = 32
CONV_ROW_TILE = 32
PROMPT_T = 256
SAMPLE_NS = 8
VMEM_LIMIT = 56 * 1024 * 1024

F32 = jnp.float32
BF16 = jnp.bfloat16


def _dot(a, b):
    return jnp.dot(a, b, preferred_element_type=F32)


def _dot_nt(a, b):
    return lax.dot_general(a, b, (((1,), (1,)), ((), ())), preferred_element_type=F32)


def _sigmoid(x):
    return 1.0 / (1.0 + jnp.exp(-x))


def _silu(x):
    return x * _sigmoid(x)


def _rms_norm(x, g):
    ms = jnp.mean(x * x, axis=-1, keepdims=True)
    return x * lax.rsqrt(ms + EPS) * g


def _layer_norm(x, g, b):
    mu = jnp.mean(x, axis=-1, keepdims=True)
    xc = x - mu
    var = jnp.mean(xc * xc, axis=-1, keepdims=True)
    return xc * lax.rsqrt(var + EPS) * g + b


def _rope_tables(pos, invf):
    ang = pos * invf
    cos = jnp.cos(ang)
    sin = jnp.sin(ang)
    d = lax.broadcasted_iota(jnp.int32, ang.shape, 1) % HEAD_DIM
    half = ROPE_DIM // 2
    s_hi = jnp.where((d >= half) & (d < ROPE_DIM), sin, 0.0)
    s_lo = jnp.where(d < half, -sin, 0.0)
    return cos, s_hi, s_lo


def _rope_tile(x, cos, s_hi, s_lo):
    half = ROPE_DIM // 2
    return x * cos + pltpu.roll(x, half, 1) * s_hi + pltpu.roll(x, LANES - half, 1) * s_lo


def _prompt_layer_kernel(x_ref, ng_ref, win_ref, cw_ref, cb_ref, lng_ref, lnb_ref, wco_ref,
                         sinks_ref, wao_ref, wout_ref, invf_ref, fng_ref,
                         y_ref, nconv_ref, nk_ref, nv_ref,
                         h_ref, full_ref, ga_ref, cc_ref, q_ref, kd_ref, vd_ref, gb_ref, og_ref,
                         *, T, final_norm):
    i = pl.program_id(1)
    last = pl.num_programs(1) - 1
    n_qblk = T // BLOCK

    x = x_ref[...]
    h_ref[...] = _rms_norm(x, ng_ref[...]).astype(BF16)

    @pl.when(i == 0)
    def _():
        full_ref[0:CONV_PAD, :] = jnp.zeros((CONV_PAD, D_MODEL), F32)
        kd_ref[:, 0:BLOCK, :] = jnp.zeros((N_KV_HEADS, BLOCK, LANES), BF16)
        vd_ref[:, :, 0:BLOCK, :] = jnp.zeros((N_KV_HEADS, 2, BLOCK, LANES), BF16)

    h = h_ref[...]
    za = _dot(h, win_ref[:, COL_GLU_A:COL_GLU_A + D_MODEL])
    zb = _dot(h, win_ref[:, COL_GLU_B:COL_GLU_B + D_MODEL])
    full_ref[CONV_PAD:CONV_PAD + T, :] = za * _sigmoid(zb)
    ga_ref[...] = _silu(_dot(h, win_ref[:, COL_GATE_A:COL_GATE_A + D_MODEL]))

    @pl.when(i == last)
    def _():
        nconv_ref[...] = full_ref[CONV_PAD + T - CONV_BUF:CONV_PAD + T, :]

    base = CONV_PAD - CONV_BUF

    for r0 in range(0, T, CONV_ROW_TILE):
        cols = []
        for c in range(N_TILES):
            lanes = slice(c * LANES, (c + 1) * LANES)
            acc = full_ref[pl.ds(r0 + base, CONV_ROW_TILE), lanes] * cw_ref[0:1, lanes]
            for j in range(1, CONV_KERNEL):
                acc = acc + full_ref[pl.ds(r0 + base + j, CONV_ROW_TILE), lanes] * cw_ref[j:j + 1, lanes]
            cols.append(acc)
        c_full = jnp.concatenate(cols, axis=1) + cb_ref[...]
        y = _layer_norm(c_full, lng_ref[...], lnb_ref[...])
        cc = _silu(y) * ga_ref[pl.ds(r0, CONV_ROW_TILE), :]
        cc_ref[pl.ds(r0, CONV_ROW_TILE), :] = cc.astype(BF16)
    full_ref[0:CONV_PAD, :] = full_ref[T:T + CONV_PAD, :]
    y_a = _dot(cc_ref[...], wco_ref[...])

    pos = (i * T + lax.broadcasted_iota(jnp.int32, (T, 1), 0)).astype(F32)
    cos, s_hi, s_lo = _rope_tables(pos, invf_ref[...])
    scale = HEAD_DIM ** -0.5
    q = _dot(h, win_ref[:, COL_Q:COL_Q + D_MODEL])
    for m in range(N_TILES):
        lanes = slice(m * LANES, (m + 1) * LANES)
        q_ref[:, lanes] = (_rope_tile(q[:, lanes], cos, s_hi, s_lo) * scale).astype(BF16)
    k = _rope_tile(_dot(h, win_ref[:, COL_K:COL_K + KV_WIDTH]), cos, s_hi, s_lo)
    v = _dot(h, win_ref[:, COL_V:COL_V + KV_WIDTH])
    gb_ref[...] = _silu(_dot(h, win_ref[:, COL_GATE_B:COL_GATE_B + D_MODEL]))

    @pl.when(i == last)
    def _():
        nk_ref[...] = k[T - WINDOW:T, :]
        nv_ref[...] = v[T - WINDOW:T, :]

    lane = lax.broadcasted_iota(jnp.int32, (T, LANES), 1)
    lo = lane < HEAD_DIM
    k_sw = pltpu.roll(k, HEAD_DIM, 1)
    v_sw = pltpu.roll(v, HEAD_DIM, 1)
    kd_ref[0, BLOCK:BLOCK + T, :] = jnp.where(lo, k, k_sw).astype(BF16)
    kd_ref[1, BLOCK:BLOCK + T, :] = jnp.where(lo, k_sw, k).astype(BF16)
    vd_ref[0, 0, BLOCK:BLOCK + T, :] = jnp.where(lo, v, 0.0).astype(BF16)
    vd_ref[0, 1, BLOCK:BLOCK + T, :] = jnp.where(lo, 0.0, v_sw).astype(BF16)
    vd_ref[1, 0, BLOCK:BLOCK + T, :] = jnp.where(lo, v_sw, 0.0).astype(BF16)
    vd_ref[1, 1, BLOCK:BLOCK + T, :] = jnp.where(lo, 0.0, v).astype(BF16)

    qi = lax.broadcasted_iota(jnp.int32, (BLOCK, 2 * BLOCK), 0)
    kj = lax.broadcasted_iota(jnp.int32, (BLOCK, 2 * BLOCK), 1)
    band = (kj - qi >= 1) & (kj - qi <= WINDOW)
    first_key = jnp.where(i > 0, 0, BLOCK)
    band_first = band & (kj >= first_key)
    lane_b = lax.broadcasted_iota(jnp.int32, (BLOCK, LANES), 1)
    lo_b = lane_b < HEAD_DIM

    for a in range(n_qblk):
        rows = slice(a * BLOCK, (a + 1) * BLOCK)
        keys = slice(a * BLOCK, a * BLOCK + 2 * BLOCK)
        mask = band_first if a == 0 else band
        for m in range(N_TILES):
            lanes = slice(m * LANES, (m + 1) * LANES)
            g = (2 * m) // GROUP
            q_tile = q_ref[rows, lanes]
            o_pair = jnp.zeros((BLOCK, LANES), F32)
            inv_pair = jnp.zeros((BLOCK, LANES), F32)
            for p in range(2):
                sink = sinks_ref[2 * m + p]
                half = lo_b if p == 0 else jnp.logical_not(lo_b)
                qm = jnp.where(half, q_tile, jnp.zeros_like(q_tile))
                s = _dot_nt(qm, kd_ref[g, keys, :])
                s = jnp.where(mask, s, NEG)
                mx = jnp.maximum(jnp.max(s, axis=-1, keepdims=True), sink)
                pr = jnp.exp(s - mx)
                den = jnp.sum(pr, axis=-1, keepdims=True) + jnp.exp(sink - mx)
                o_pair = o_pair + _dot(pr.astype(BF16), vd_ref[g, p, keys, :])
                inv_pair = jnp.where(half, 1.0 / den, inv_pair)
            og_ref[rows, lanes] = (o_pair * inv_pair * gb_ref[rows, lanes]).astype(BF16)

    kd_ref[:, 0:BLOCK, :] = kd_ref[:, T:T + BLOCK, :]
    vd_ref[:, :, 0:BLOCK, :] = vd_ref[:, :, T:T + BLOCK, :]
    y_b = _dot(og_ref[...], wao_ref[...])

    mg_a = _sigmoid(_dot(h, win_ref[:, COL_MG_A:COL_MG_A + D_MODEL]))
    mg_b = _sigmoid(_dot(h, win_ref[:, COL_MG_B:COL_MG_B + D_MODEL]))
    y = (mg_a * y_a + mg_b * y_b).astype(BF16)
    out = x + _dot(y, wout_ref[...])
    if final_norm:
        out = _rms_norm(out, fng_ref[...])
    y_ref[...] = out


def _full_spec(shape):
    nd = len(shape)
    return pl.BlockSpec(shape, lambda *_: (0,) * nd)


def _prompt_layer(x, ng, win, cw, cb, lng, lnb, wco, sinks, wao, wout, invf, fng, *, final_norm):
    B, S, _ = x.shape
    T = PROMPT_T
    kernel = functools.partial(_prompt_layer_kernel, T=T, final_norm=final_norm)
    row = (1, D_MODEL)
    in_specs = [
        pl.BlockSpec((None, T, D_MODEL), lambda b, i: (b, i, 0)),
        _full_spec(row),
        _full_spec((D_MODEL, IN_COLS)),
        _full_spec((CONV_KERNEL, D_MODEL)),
        _full_spec(row), _full_spec(row), _full_spec(row),
        _full_spec((D_MODEL, D_MODEL)),
        pl.BlockSpec(memory_space=pltpu.SMEM),
        _full_spec((D_MODEL, D_MODEL)),
        _full_spec((D_MODEL, D_MODEL)),
        _full_spec((1, LANES)),
        _full_spec(row),
    ]
    out_specs = [
        pl.BlockSpec((None, T, D_MODEL), lambda b, i: (b, i, 0)),
        pl.BlockSpec((None, CONV_BUF, D_MODEL), lambda b, i: (b, 0, 0)),
        pl.BlockSpec((None, WINDOW, KV_WIDTH), lambda b, i: (b, 0, 0)),
        pl.BlockSpec((None, WINDOW, KV_WIDTH), lambda b, i: (b, 0, 0)),
    ]
    out_shape = [
        jax.ShapeDtypeStruct((B, S, D_MODEL), F32),
        jax.ShapeDtypeStruct((B, CONV_BUF, D_MODEL), F32),
        jax.ShapeDtypeStruct((B, WINDOW, KV_WIDTH), F32),
        jax.ShapeDtypeStruct((B, WINDOW, KV_WIDTH), F32),
    ]
    scratch = [
        pltpu.VMEM((T, D_MODEL), BF16),
        pltpu.VMEM((CONV_PAD + T, D_MODEL), F32),
        pltpu.VMEM((T, D_MODEL), F32),
        pltpu.VMEM((T, D_MODEL), BF16),
        pltpu.VMEM((T, D_MODEL), BF16),
        pltpu.VMEM((N_KV_HEADS, BLOCK + T, LANES), BF16),
        pltpu.VMEM((N_KV_HEADS, 2, BLOCK + T, LANES), BF16),
        pltpu.VMEM((T, D_MODEL), F32),
        pltpu.VMEM((T, D_MODEL), BF16),
    ]
    return pl.pallas_call(
        kernel,
        grid=(B, S // T),
        in_specs=in_specs,
        out_specs=out_specs,
        out_shape=out_shape,
        scratch_shapes=scratch,
        compiler_params=pltpu.CompilerParams(
            dimension_semantics=("arbitrary", "arbitrary"),
            vmem_limit_bytes=VMEM_LIMIT),
        name="prompt_layer",
    )(x, ng, win, cw, cb, lng, lnb, wco, sinks, wao, wout, invf, fng)


def _sample_layer_kernel(x_ref, st_ref, ck_ref, cv_ref, ng_ref, win_ref, cw_ref, cb_ref, lng_ref,
                         lnb_ref, wco_ref, sinks_ref, wao_ref, wout_ref, invf_ref, fng_ref,
                         y_ref, nconv_ref, nk_ref, nv_ref,
                         h_ref, u_ref, q_ref, k_ref, v_ref, c_ref, o_ref, qt_ref, ot_ref,
                         cblk_ref, oblk_ref,
                         *, NS, final_norm):
    s = pl.program_id(0)
    last = pl.num_programs(0) - 1
    N = x_ref.shape[0]

    @pl.when(s == 0)
    def _():
        h = _rms_norm(x_ref[...], ng_ref[...]).astype(BF16)
        h_ref[...] = h
        za = _dot(h, win_ref[:, COL_GLU_A:COL_GLU_A + D_MODEL])
        zb = _dot(h, win_ref[:, COL_GLU_B:COL_GLU_B + D_MODEL])
        u_ref[...] = za * _sigmoid(zb)
        pos = jnp.full((N, 1), float(PAST_LEN), F32)
        cos, s_hi, s_lo = _rope_tables(pos, invf_ref[...])
        scale = HEAD_DIM ** -0.5
        q = _dot(h, win_ref[:, COL_Q:COL_Q + D_MODEL])
        for m in range(N_TILES):
            lanes = slice(m * LANES, (m + 1) * LANES)
            q_ref[:, lanes] = _rope_tile(q[:, lanes], cos, s_hi, s_lo) * scale
        k_ref[...] = _rope_tile(_dot(h, win_ref[:, COL_K:COL_K + KV_WIDTH]), cos, s_hi, s_lo)
        v_ref[...] = _dot(h, win_ref[:, COL_V:COL_V + KV_WIDTH])

    sink_col = sinks_ref[...]
    row8 = lax.broadcasted_iota(jnp.int32, (N_TILES, LANES), 0)
    lane8 = lax.broadcasted_iota(jnp.int32, (N_TILES, LANES), 1)
    lo8 = lane8 < HEAD_DIM
    g0 = row8 < (N_TILES // 2)

    blk = pl.ds(pl.multiple_of(s * NS, NS), NS)
    u_blk = u_ref[blk, :]
    q_blk = q_ref[blk, :]
    k_blk = k_ref[blk, :]
    v_blk = v_ref[blk, :]
    for n in range(NS):
        st = st_ref[n]
        u_row = u_blk[n:n + 1, :]
        cblk_ref[n:n + 1, :] = (jnp.sum(st * cw_ref[0:CONV_BUF, :], axis=0, keepdims=True)
                                + u_row * cw_ref[CONV_BUF:CONV_KERNEL, :] + cb_ref[...])
        nconv_ref[n, 0:CONV_BUF - 1, :] = st_ref[n, 1:CONV_BUF, :]
        nconv_ref[n, CONV_BUF - 1:CONV_BUF, :] = u_row
        nk_ref[n, 0:WINDOW - 1, :] = ck_ref[n, 1:WINDOW, :]
        nk_ref[n, WINDOW - 1:WINDOW, :] = k_blk[n:n + 1, :]
        nv_ref[n, 0:WINDOW - 1, :] = cv_ref[n, 1:WINDOW, :]
        nv_ref[n, WINDOW - 1:WINDOW, :] = v_blk[n:n + 1, :]
        for m in range(N_TILES):
            qt_ref[m:m + 1, :] = q_blk[n:n + 1, m * LANES:(m + 1) * LANES]
        qt = qt_ref[...]
        qe = jnp.where(lo8, qt, 0.0)
        qo = jnp.where(lo8, 0.0, qt)
        q_even = jnp.where(g0, qe, pltpu.roll(qe, HEAD_DIM, 1))
        q_odd = jnp.where(g0, pltpu.roll(qo, HEAD_DIM, 1), qo)
        lhs = jnp.concatenate([q_even, q_odd], axis=0).astype(BF16)
        kn = nk_ref[n].astype(BF16)
        vn = nv_ref[n].astype(BF16)
        sc = _dot_nt(lhs, kn)
        mx = jnp.maximum(jnp.max(sc, axis=-1, keepdims=True), sink_col)
        pr = jnp.exp(sc - mx)
        den = jnp.sum(pr, axis=-1, keepdims=True) + jnp.exp(sink_col - mx)
        o = _dot(pr.astype(BF16), vn) * (1.0 / den)
        o_even = o[0:N_TILES, :]
        o_odd = o[N_TILES:N_HEADS, :]
        oe = jnp.where(g0, o_even, pltpu.roll(o_even, HEAD_DIM, 1))
        oo = jnp.where(g0, pltpu.roll(o_odd, HEAD_DIM, 1), o_odd)
        ot_ref[...] = jnp.where(lo8, oe, oo)
        for m in range(N_TILES):
            oblk_ref[n:n + 1, m * LANES:(m + 1) * LANES] = ot_ref[m:m + 1, :]
    c_ref[blk, :] = cblk_ref[...]
    o_ref[blk, :] = oblk_ref[...]

    @pl.when(s == last)
    def _():
        h = h_ref[...]
        x = x_ref[...]
        ga = _silu(_dot(h, win_ref[:, COL_GATE_A:COL_GATE_A + D_MODEL]))
        cc = _silu(_layer_norm(c_ref[...], lng_ref[...], lnb_ref[...])) * ga
        y_a = _dot(cc.astype(BF16), wco_ref[...])
        gb = _silu(_dot(h, win_ref[:, COL_GATE_B:COL_GATE_B + D_MODEL]))
        y_b = _dot((o_ref[...] * gb).astype(BF16), wao_ref[...])
        mg_a = _sigmoid(_dot(h, win_ref[:, COL_MG_A:COL_MG_A + D_MODEL]))
        mg_b = _sigmoid(_dot(h, win_ref[:, COL_MG_B:COL_MG_B + D_MODEL]))
        y = (mg_a * y_a + mg_b * y_b).astype(BF16)
        out = x + _dot(y, wout_ref[...])
        if final_norm:
            out = _rms_norm(out, fng_ref[...])
        y_ref[...] = out


def _sample_layer(x, st, ck, cv, ng, win, cw, cb, lng, lnb, wco, sinks, wao, wout, invf, fng,
                  *, layer, final_norm):
    N = x.shape[0]
    NS = SAMPLE_NS
    kernel = functools.partial(_sample_layer_kernel, NS=NS, final_norm=final_norm)
    row = (1, D_MODEL)
    in_specs = [
        _full_spec((N, D_MODEL)),
        pl.BlockSpec((None, NS, CONV_BUF, D_MODEL), lambda s: (layer, s, 0, 0)),
        pl.BlockSpec((None, NS, WINDOW, KV_WIDTH), lambda s: (layer, s, 0, 0)),
        pl.BlockSpec((None, NS, WINDOW, KV_WIDTH), lambda s: (layer, s, 0, 0)),
        _full_spec(row),
        _full_spec((D_MODEL, IN_COLS)),
        _full_spec((CONV_KERNEL, D_MODEL)),
        _full_spec(row), _full_spec(row), _full_spec(row),
        _full_spec((D_MODEL, D_MODEL)),
        _full_spec((N_HEADS, 1)),
        _full_spec((D_MODEL, D_MODEL)),
        _full_spec((D_MODEL, D_MODEL)),
        _full_spec((1, LANES)),
        _full_spec(row),
    ]
    out_specs = [
        _full_spec((N, D_MODEL)),
        pl.BlockSpec((NS, CONV_BUF, D_MODEL), lambda s: (s, 0, 0)),
        pl.BlockSpec((NS, WINDOW, KV_WIDTH), lambda s: (s, 0, 0)),
        pl.BlockSpec((NS, WINDOW, KV_WIDTH), lambda s: (s, 0, 0)),
    ]
    out_shape = [
        jax.ShapeDtypeStruct((N, D_MODEL), F32),
        jax.ShapeDtypeStruct((N, CONV_BUF, D_MODEL), F32),
        jax.ShapeDtypeStruct((N, WINDOW, KV_WIDTH), F32),
        jax.ShapeDtypeStruct((N, WINDOW, KV_WIDTH), F32),
    ]
    scratch = [
        pltpu.VMEM((N, D_MODEL), BF16),
        pltpu.VMEM((N, D_MODEL), F32),
        pltpu.VMEM((N, D_MODEL), F32),
        pltpu.VMEM((N, KV_WIDTH), F32),
        pltpu.VMEM((N, KV_WIDTH), F32),
        pltpu.VMEM((N, D_MODEL), F32),
        pltpu.VMEM((N, D_MODEL), F32),
        pltpu.VMEM((N_TILES, LANES), F32),
        pltpu.VMEM((N_TILES, LANES), F32),
        pltpu.VMEM((NS, D_MODEL), F32),
        pltpu.VMEM((NS, D_MODEL), F32),
    ]
    return pl.pallas_call(
        kernel,
        grid=(N // NS,),
        in_specs=in_specs,
        out_specs=out_specs,
        out_shape=out_shape,
        scratch_shapes=scratch,
        compiler_params=pltpu.CompilerParams(
            dimension_semantics=("arbitrary",),
            vmem_limit_bytes=VMEM_LIMIT),
        name="sample_layer",
    )(x, st, ck, cv, ng, win, cw, cb, lng, lnb, wco, sinks, wao, wout, invf, fng)


def _inv_freq_lanes():
    inv = ROPE_THETA ** (-jnp.arange(0, ROPE_DIM, 2, dtype=F32) / ROPE_DIM)
    d = jnp.arange(LANES) % HEAD_DIM
    return jnp.where(d < ROPE_DIM, inv[d % (ROPE_DIM // 2)], 0.0).astype(F32)[None, :]


def kernel(x_prompt, x_sample, state_conv, cache_k_win, cache_v_win, norm_g, w_in, conv_w, conv_b,
           conv_ln_g, conv_ln_b, w_conv_out, attn_sinks, w_attn_out, w_out, final_norm_g):
    depth = w_in.shape[0]
    n_batch = x_prompt.shape[0]
    n_dec = x_sample.shape[0]
    invf = _inv_freq_lanes()
    fng = final_norm_g[None, :]
    hp = x_prompt
    hs = x_sample[:, 0, :]
    ck = cache_k_win.reshape(depth, n_dec, WINDOW, KV_WIDTH)
    cv = cache_v_win.reshape(depth, n_dec, WINDOW, KV_WIDTH)
    outs_p, outs_s = [], []
    for l in range(depth):
        head = (norm_g[l][None, :], w_in[l].astype(BF16), conv_w[l], conv_b[l][None, :],
                conv_ln_g[l][None, :], conv_ln_b[l][None, :], w_conv_out[l].astype(BF16))
        tail = (w_attn_out[l].astype(BF16), w_out[l].astype(BF16), invf, fng)
        sinks_col = attn_sinks[l].reshape(N_TILES, 2).T.reshape(N_HEADS, 1)
        final = l == depth - 1
        hp, c1, k1, v1 = _prompt_layer(hp, *head, attn_sinks[l], *tail, final_norm=final)
        hs, c2, k2, v2 = _sample_layer(hs, state_conv, ck, cv, *head, sinks_col, *tail,
                                       layer=l, final_norm=final)
        outs_p.append((c1, k1, v1))
        outs_s.append((c2, k2, v2))
    kv_p = (depth, n_batch, WINDOW, N_KV_HEADS, HEAD_DIM)
    kv_s = (depth, n_dec, WINDOW, N_KV_HEADS, HEAD_DIM)
    return (hp, hs[:, None, :],
            jnp.stack([o[0] for o in outs_p]),
            jnp.stack([o[1] for o in outs_p]).reshape(kv_p),
            jnp.stack([o[2] for o in outs_p]).reshape(kv_p),
            jnp.stack([o[0] for o in outs_s]),
            jnp.stack([o[1] for o in outs_s]).reshape(kv_s),
            jnp.stack([o[2] for o in outs_s]).reshape(kv_s))
```

```python
import functools
import math

import jax
import jax.numpy as jnp
from jax import lax
from jax.experimental import pallas as pl
from jax.experimental.pallas import tpu as pltpu

D_MODEL = 1024
CONV_KERNEL = 31
CONV_BUF = CONV_KERNEL - 1
N_HEADS = 16
HEAD_DIM = 64
N_KV_HEADS = 2
GROUP = N_HEADS // N_KV_HEADS
KV_WIDTH = N_KV_HEADS * HEAD_DIM
WINDOW = 128
BLOCK = 128
ROPE_DIM = HEAD_DIM // 4
ROPE_THETA = 500000.0
EPS = 1e-6
NEG = -1e30
PAST_LEN = 16384

COL_GLU_A = 0
COL_GLU_B = 1024
COL_GATE_A = 2048
COL_Q = 3072
COL_K = 4096
COL_V = 4224
COL_GATE_B = 4352
COL_MG_A = 5376
COL_MG_B = 6400
IN_COLS = 7424

LANES = 128
N_TILES = D_MODEL // LANES
CONV_PAD = 32
CONV_ROW_TILE = 32
PROMPT_T = 256
SAMPLE_NS = 8
VMEM_LIMIT = 56 * 1024 * 1024

F32 = jnp.float32
BF16 = jnp.bfloat16


def _dot(a, b):
    return jnp.dot(a, b, preferred_element_type=F32)


def _dot_nt(a, b):
    return lax.dot_general(a, b, (((1,), (1,)), ((), ())), preferred_element_type=F32)


def _sigmoid(x):
    return 0.5 * jnp.tanh(0.5 * x) + 0.5


def _silu(x):
    return x * _sigmoid(x)


def _rms_norm(x, g):
    ms = jnp.mean(x * x, axis=-1, keepdims=True)
    return x * lax.rsqrt(ms + EPS) * g


def _layer_norm(x, g, b):
    mu = jnp.mean(x, axis=-1, keepdims=True)
    xc = x - mu
    var = jnp.mean(xc * xc, axis=-1, keepdims=True)
    return xc * lax.rsqrt(var + EPS) * g + b


def _rope_tables(pos, invf):
    ang = pos * invf
    cos = jnp.cos(ang)
    sin = jnp.sin(ang)
    d = lax.broadcasted_iota(jnp.int32, ang.shape, 1) % HEAD_DIM
    half = ROPE_DIM // 2
    s_hi = jnp.where((d >= half) & (d < ROPE_DIM), sin, 0.0)
    s_lo = jnp.where(d < half, -sin, 0.0)
    return cos, s_hi, s_lo


def _rope_tile(x, cos, s_hi, s_lo):
    half = ROPE_DIM // 2
    return x * cos + pltpu.roll(x, half, 1) * s_hi + pltpu.roll(x, LANES - half, 1) * s_lo


def _prompt_layer_kernel(x_ref, ng_ref, win_ref, cw_ref, cb_ref, lng_ref, lnb_ref, wco_ref,
                         sinks_ref, wao_ref, wout_ref, invf_ref, fng_ref,
                         y_ref, nconv_ref, nk_ref, nv_ref,
                         h_ref, full_ref, ga_ref, cc_ref, q_ref, kd_ref, vd_ref, gb_ref, og_ref,
                         mga_ref, mgb_ref, *, T, final_norm):
    i = pl.program_id(1)
    n_qblk = T // BLOCK

    @pl.when(i == 0)
    def _():
        full_ref[:, 0:CONV_PAD, :] = jnp.zeros((N_TILES, CONV_PAD, LANES), F32)
        kd_ref[:, 0:BLOCK, :] = jnp.zeros((N_KV_HEADS, BLOCK, LANES), BF16)
        vd_ref[:, :, 0:BLOCK, :] = jnp.zeros((N_KV_HEADS, 2, BLOCK, LANES), BF16)

    h_ref[...] = _rms_norm(x_ref[...], ng_ref[...]).astype(BF16)

    def in_proj(col, width=D_MODEL):
        return _dot(h_ref[...], win_ref[:, col:col + width])

    base = CONV_PAD - CONV_BUF
    half_rows = CONV_ROW_TILE // 2

    def conv_group(r0):
        acc = [[None] * N_TILES, [None] * N_TILES]
        for c in range(N_TILES):
            lanes = slice(c * LANES, (c + 1) * LANES)
            for j in range(CONV_KERNEL):
                w = cw_ref[j:j + 1, lanes]
                for par in range(2):
                    tap = full_ref[c, pl.ds(r0 + base + j + par, half_rows, stride=2), :] * w
                    acc[par][c] = tap if j == 0 else acc[par][c] + tap
        for par in range(2):
            rows = pl.ds(r0 + par, half_rows, stride=2)
            c_rows = jnp.concatenate(acc[par], axis=1) + cb_ref[...]
            y = _layer_norm(c_rows, lng_ref[...], lnb_ref[...])
            g_rows = jnp.concatenate([ga_ref[c, rows, :] for c in range(N_TILES)], axis=1)
            cc = _silu(y) * g_rows
            for c in range(N_TILES):
                cc_ref[c, rows, :] = cc[:, c * LANES:(c + 1) * LANES]

    pos = (i * T + lax.broadcasted_iota(jnp.int32, (T, 1), 0)).astype(F32)
    cos, s_hi, s_lo = _rope_tables(pos, invf_ref[...])
    scale = HEAD_DIM ** -0.5

    u = in_proj(COL_GLU_A) * _sigmoid(in_proj(COL_GLU_B))
    for c in range(N_TILES):
        full_ref[c, CONV_PAD:CONV_PAD + T, :] = u[:, c * LANES:(c + 1) * LANES]
    nconv_ref[...] = u[T - CONV_BUF:T, :]
    ga = _silu(in_proj(COL_GATE_A))
    for c in range(N_TILES):
        ga_ref[c] = ga[:, c * LANES:(c + 1) * LANES]

    q = in_proj(COL_Q)
    for m in range(N_TILES):
        lanes = slice(m * LANES, (m + 1) * LANES)
        q_ref[:, lanes] = (_rope_tile(q[:, lanes], cos, s_hi, s_lo) * scale).astype(BF16)
    k = _rope_tile(in_proj(COL_K, KV_WIDTH), cos, s_hi, s_lo)
    v = in_proj(COL_V, KV_WIDTH)
    nk_ref[...] = k[T - WINDOW:T, :]
    nv_ref[...] = v[T - WINDOW:T, :]
    lane = lax.broadcasted_iota(jnp.int32, (T, LANES), 1)
    lo = lane < HEAD_DIM
    k_sw = pltpu.roll(k, HEAD_DIM, 1)
    v_sw = pltpu.roll(v, HEAD_DIM, 1)
    kd_ref[0, BLOCK:BLOCK + T, :] = jnp.where(lo, k, k_sw).astype(BF16)
    kd_ref[1, BLOCK:BLOCK + T, :] = jnp.where(lo, k_sw, k).astype(BF16)
    vd_ref[0, 0, BLOCK:BLOCK + T, :] = jnp.where(lo, v, 0.0).astype(BF16)
    vd_ref[0, 1, BLOCK:BLOCK + T, :] = jnp.where(lo, 0.0, v_sw).astype(BF16)
    vd_ref[1, 0, BLOCK:BLOCK + T, :] = jnp.where(lo, v_sw, 0.0).astype(BF16)
    vd_ref[1, 1, BLOCK:BLOCK + T, :] = jnp.where(lo, 0.0, v).astype(BF16)
    gb_ref[...] = _silu(in_proj(COL_GATE_B))
    mga_ref[...] = _sigmoid(in_proj(COL_MG_A))
    mgb_ref[...] = _sigmoid(in_proj(COL_MG_B))

    qi = lax.broadcasted_iota(jnp.int32, (BLOCK, 2 * BLOCK), 0)
    kj = lax.broadcasted_iota(jnp.int32, (BLOCK, 2 * BLOCK), 1)
    band = (kj - qi >= 1) & (kj - qi <= WINDOW)
    first_key = jnp.where(i > 0, 0, BLOCK)
    band_first = band & (kj >= first_key)
    lane_b = lax.broadcasted_iota(jnp.int32, (BLOCK, LANES), 1)
    lo_b = lane_b < HEAD_DIM

    def attention_tile(a, m):
        rows = slice(a * BLOCK, (a + 1) * BLOCK)
        keys = slice(a * BLOCK, a * BLOCK + 2 * BLOCK)
        mask = band_first if a == 0 else band
        lanes = slice(m * LANES, (m + 1) * LANES)
        g = (2 * m) // GROUP
        q_tile = q_ref[rows, lanes]
        o_pair = jnp.zeros((BLOCK, LANES), F32)
        inv_pair = jnp.zeros((BLOCK, LANES), F32)
        for p in range(2):
            sink = sinks_ref[2 * m + p]
            half = lo_b if p == 0 else jnp.logical_not(lo_b)
            qm = jnp.where(half, q_tile, jnp.zeros_like(q_tile))
            s = _dot_nt(qm, kd_ref[g, keys, :])
            s = jnp.where(mask, s, NEG)
            mx = jnp.maximum(jnp.max(s, axis=-1, keepdims=True), sink)
            pr = jnp.exp(s - mx)
            den = jnp.sum(pr, axis=-1, keepdims=True) + jnp.exp(sink - mx)
            o_pair = o_pair + _dot(pr.astype(BF16), vd_ref[g, p, keys, :])
            inv_pair = jnp.where(half, 1.0 / den, inv_pair)
        og_ref[rows, lanes] = (o_pair * inv_pair * gb_ref[rows, lanes]).astype(BF16)

    for a in range(n_qblk):
        for m in range(N_TILES):
            attention_tile(a, m)

    kd_ref[:, 0:BLOCK, :] = kd_ref[:, T:T + BLOCK, :]
    vd_ref[:, :, 0:BLOCK, :] = vd_ref[:, :, T:T + BLOCK, :]

    for r0 in range(0, T, CONV_ROW_TILE):
        conv_group(r0)
    full_ref[:, 0:CONV_PAD, :] = full_ref[:, T:T + CONV_PAD, :]

    cc_all = jnp.concatenate([cc_ref[c] for c in range(N_TILES)], axis=1).astype(BF16)
    y_a = _dot(cc_all, wco_ref[...])
    y_b = _dot(og_ref[...], wao_ref[...])
    y = (mga_ref[...] * y_a + mgb_ref[...] * y_b).astype(BF16)
    out = x_ref[...] + _dot(y, wout_ref[...])
    if final_norm:
        out = _rms_norm(out, fng_ref[...])
    y_ref[...] = out


def _full_spec(shape):
    nd = len(shape)
    return pl.BlockSpec(shape, lambda *_: (0,) * nd)


def _prompt_layer(x, ng, win, cw, cb, lng, lnb, wco, sinks, wao, wout, invf, fng, *, final_norm):
    B, S, _ = x.shape
    T = PROMPT_T
    kernel = functools.partial(_prompt_layer_kernel, T=T, final_norm=final_norm)
    row = (1, D_MODEL)
    in_specs = [
        pl.BlockSpec((None, T, D_MODEL), lambda b, i: (b, i, 0)),
        _full_spec(row),
        _full_spec((D_MODEL, IN_COLS)),
        _full_spec((CONV_KERNEL, D_MODEL)),
        _full_spec(row), _full_spec(row), _full_spec(row),
        _full_spec((D_MODEL, D_MODEL)),
        pl.BlockSpec(memory_space=pltpu.SMEM),
        _full_spec((D_MODEL, D_MODEL)),
        _full_spec((D_MODEL, D_MODEL)),
        _full_spec((1, LANES)),
        _full_spec(row),
    ]
    out_specs = [
        pl.BlockSpec((None, T, D_MODEL), lambda b, i: (b, i, 0)),
        pl.BlockSpec((None, CONV_BUF, D_MODEL), lambda b, i: (b, 0, 0)),
        pl.BlockSpec((None, WINDOW, KV_WIDTH), lambda b, i: (b, 0, 0)),
        pl.BlockSpec((None, WINDOW, KV_WIDTH), lambda b, i: (b, 0, 0)),
    ]
    out_shape = [
        jax.ShapeDtypeStruct((B, S, D_MODEL), F32),
        jax.ShapeDtypeStruct((B, CONV_BUF, D_MODEL), F32),
        jax.ShapeDtypeStruct((B, WINDOW, KV_WIDTH), F32),
        jax.ShapeDtypeStruct((B, WINDOW, KV_WIDTH), F32),
    ]
    scratch = [
        pltpu.VMEM((T, D_MODEL), BF16),
        pltpu.VMEM((N_TILES, CONV_PAD + T, LANES), F32),
        pltpu.VMEM((N_TILES, T, LANES), F32),
        pltpu.VMEM((N_TILES, T, LANES), F32),
        pltpu.VMEM((T, D_MODEL), BF16),
        pltpu.VMEM((N_KV_HEADS, BLOCK + T, LANES), BF16),
        pltpu.VMEM((N_KV_HEADS, 2, BLOCK + T, LANES), BF16),
        pltpu.VMEM((T, D_MODEL), F32),
        pltpu.VMEM((T, D_MODEL), BF16),
        pltpu.VMEM((T, D_MODEL), F32),
        pltpu.VMEM((T, D_MODEL), F32),
    ]
    return pl.pallas_call(
        kernel,
        grid=(B, S // T),
        in_specs=in_specs,
        out_specs=out_specs,
        out_shape=out_shape,
        scratch_shapes=scratch,
        compiler_params=pltpu.CompilerParams(
            dimension_semantics=("arbitrary", "arbitrary"),
            vmem_limit_bytes=VMEM_LIMIT),
        name="prompt_layer",
    )(x, ng, win, cw, cb, lng, lnb, wco, sinks, wao, wout, invf, fng)


def _sample_layer_kernel(x_ref, st_ref, ck_ref, cv_ref, ng_ref, win_ref, cw_ref, cb_ref, lng_ref,
                         lnb_ref, wco_ref, sinks_ref, wao_ref, wout_ref, invf_ref, fng_ref,
                         y_ref, nconv_ref, nk_ref, nv_ref,
                         h_ref, u_ref, q_ref, k_ref, v_ref, c_ref, o_ref, qt_ref, ot_ref,
                         cblk_ref, oblk_ref,
                         *, NS, final_norm):
    s = pl.program_id(0)
    last = pl.num_programs(0) - 1
    N = x_ref.shape[0]

    @pl.when(s == 0)
    def _():
        h = _rms_norm(x_ref[...], ng_ref[...]).astype(BF16)
        h_ref[...] = h
        za = _dot(h, win_ref[:, COL_GLU_A:COL_GLU_A + D_MODEL])
        zb = _dot(h, win_ref[:, COL_GLU_B:COL_GLU_B + D_MODEL])
        u_ref[...] = za * _sigmoid(zb)
        pos = jnp.full((N, 1), float(PAST_LEN), F32)
        cos, s_hi, s_lo = _rope_tables(pos, invf_ref[...])
        scale = HEAD_DIM ** -0.5
        q = _dot(h, win_ref[:, COL_Q:COL_Q + D_MODEL])
        for m in range(N_TILES):
            lanes = slice(m * LANES, (m + 1) * LANES)
            q_ref[:, lanes] = _rope_tile(q[:, lanes], cos, s_hi, s_lo) * scale
        k_ref[...] = _rope_tile(_dot(h, win_ref[:, COL_K:COL_K + KV_WIDTH]), cos, s_hi, s_lo)
        v_ref[...] = _dot(h, win_ref[:, COL_V:COL_V + KV_WIDTH])

    sink_col = sinks_ref[...]
    row8 = lax.broadcasted_iota(jnp.int32, (N_TILES, LANES), 0)
    lane8 = lax.broadcasted_iota(jnp.int32, (N_TILES, LANES), 1)
    lo8 = lane8 < HEAD_DIM
    g0 = row8 < (N_TILES // 2)

    blk = pl.ds(pl.multiple_of(s * NS, NS), NS)
    u_blk = u_ref[blk, :]
    q_blk = q_ref[blk, :]
    k_blk = k_ref[blk, :]
    v_blk = v_ref[blk, :]
    for n in range(NS):
        st = st_ref[n]
        u_row = u_blk[n:n + 1, :]
        cblk_ref[n:n + 1, :] = (jnp.sum(st * cw_ref[0:CONV_BUF, :], axis=0, keepdims=True)
                                + u_row * cw_ref[CONV_BUF:CONV_KERNEL, :] + cb_ref[...])
        nconv_ref[n, 0:CONV_BUF - 1, :] = st_ref[n, 1:CONV_BUF, :]
        nconv_ref[n, CONV_BUF - 1:CONV_BUF, :] = u_row
        nk_ref[n, 0:WINDOW - 1, :] = ck_ref[n, 1:WINDOW, :]
        nk_ref[n, WINDOW - 1:WINDOW, :] = k_blk[n:n + 1, :]
        nv_ref[n, 0:WINDOW - 1, :] = cv_ref[n, 1:WINDOW, :]
        nv_ref[n, WINDOW - 1:WINDOW, :] = v_blk[n:n + 1, :]
        for m in range(N_TILES):
            qt_ref[m:m + 1, :] = q_blk[n:n + 1, m * LANES:(m + 1) * LANES]
        qt = qt_ref[...]
        qe = jnp.where(lo8, qt, 0.0)
        qo = jnp.where(lo8, 0.0, qt)
        q_even = jnp.where(g0, qe, pltpu.roll(qe, HEAD_DIM, 1))
        q_odd = jnp.where(g0, pltpu.roll(qo, HEAD_DIM, 1), qo)
        lhs = jnp.concatenate([q_even, q_odd], axis=0).astype(BF16)
        kn = nk_ref[n].astype(BF16)
        vn = nv_ref[n].astype(BF16)
        sc = _dot_nt(lhs, kn)
        mx = jnp.maximum(jnp.max(sc, axis=-1, keepdims=True), sink_col)
        pr = jnp.exp(sc - mx)
        den = jnp.sum(pr, axis=-1, keepdims=True) + jnp.exp(sink_col - mx)
        o = _dot(pr.astype(BF16), vn) * (1.0 / den)
        o_even = o[0:N_TILES, :]
        o_odd = o[N_TILES:N_HEADS, :]
        oe = jnp.where(g0, o_even, pltpu.roll(o_even, HEAD_DIM, 1))
        oo = jnp.where(g0, pltpu.roll(o_odd, HEAD_DIM, 1), o_odd)
        ot_ref[...] = jnp.where(lo8, oe, oo)
        for m in range(N_TILES):
            oblk_ref[n:n + 1, m * LANES:(m + 1) * LANES] = ot_ref[m:m + 1, :]
    c_ref[blk, :] = cblk_ref[...]
    o_ref[blk, :] = oblk_ref[...]

    @pl.when(s == last)
    def _():
        h = h_ref[...]
        x = x_ref[...]
        ga = _silu(_dot(h, win_ref[:, COL_GATE_A:COL_GATE_A + D_MODEL]))
        cc = _silu(_layer_norm(c_ref[...], lng_ref[...], lnb_ref[...])) * ga
        y_a = _dot(cc.astype(BF16), wco_ref[...])
        gb = _silu(_dot(h, win_ref[:, COL_GATE_B:COL_GATE_B + D_MODEL]))
        y_b = _dot((o_ref[...] * gb).astype(BF16), wao_ref[...])
        mg_a = _sigmoid(_dot(h, win_ref[:, COL_MG_A:COL_MG_A + D_MODEL]))
        mg_b = _sigmoid(_dot(h, win_ref[:, COL_MG_B:COL_MG_B + D_MODEL]))
        y = (mg_a * y_a + mg_b * y_b).astype(BF16)
        out = x + _dot(y, wout_ref[...])
        if final_norm:
            out = _rms_norm(out, fng_ref[...])
        y_ref[...] = out


def _sample_layer(x, st, ck, cv, ng, win, cw, cb, lng, lnb, wco, sinks, wao, wout, invf, fng,
                  *, layer, final_norm):
    N = x.shape[0]
    NS = SAMPLE_NS
    kernel = functools.partial(_sample_layer_kernel, NS=NS, final_norm=final_norm)
    row = (1, D_MODEL)
    in_specs = [
        _full_spec((N, D_MODEL)),
        pl.BlockSpec((None, NS, CONV_BUF, D_MODEL), lambda s: (layer, s, 0, 0)),
        pl.BlockSpec((None, NS, WINDOW, KV_WIDTH), lambda s: (layer, s, 0, 0)),
        pl.BlockSpec((None, NS, WINDOW, KV_WIDTH), lambda s: (layer, s, 0, 0)),
        _full_spec(row),
        _full_spec((D_MODEL, IN_COLS)),
        _full_spec((CONV_KERNEL, D_MODEL)),
        _full_spec(row), _full_spec(row), _full_spec(row),
        _full_spec((D_MODEL, D_MODEL)),
        _full_spec((N_HEADS, 1)),
        _full_spec((D_MODEL, D_MODEL)),
        _full_spec((D_MODEL, D_MODEL)),
        _full_spec((1, LANES)),
        _full_spec(row),
    ]
    out_specs = [
        _full_spec((N, D_MODEL)),
        pl.BlockSpec((NS, CONV_BUF, D_MODEL), lambda s: (s, 0, 0)),
        pl.BlockSpec((NS, WINDOW, KV_WIDTH), lambda s: (s, 0, 0)),
        pl.BlockSpec((NS, WINDOW, KV_WIDTH), lambda s: (s, 0, 0)),
    ]
    out_shape = [
        jax.ShapeDtypeStruct((N, D_MODEL), F32),
        jax.ShapeDtypeStruct((N, CONV_BUF, D_MODEL), F32),
        jax.ShapeDtypeStruct((N, WINDOW, KV_WIDTH), F32),
        jax.ShapeDtypeStruct((N, WINDOW, KV_WIDTH), F32),
    ]
    scratch = [
        pltpu.VMEM((N, D_MODEL), BF16),
        pltpu.VMEM((N, D_MODEL), F32),
        pltpu.VMEM((N, D_MODEL), F32),
        pltpu.VMEM((N, KV_WIDTH), F32),
        pltpu.VMEM((N, KV_WIDTH), F32),
        pltpu.VMEM((N, D_MODEL), F32),
        pltpu.VMEM((N, D_MODEL), F32),
        pltpu.VMEM((N_TILES, LANES), F32),
        pltpu.VMEM((N_TILES, LANES), F32),
        pltpu.VMEM((NS, D_MODEL), F32),
        pltpu.VMEM((NS, D_MODEL), F32),
    ]
    return pl.pallas_call(
        kernel,
        grid=(N // NS,),
        in_specs=in_specs,
        out_specs=out_specs,
        out_shape=out_shape,
        scratch_shapes=scratch,
        compiler_params=pltpu.CompilerParams(
            dimension_semantics=("arbitrary",),
            vmem_limit_bytes=VMEM_LIMIT),
        name="sample_layer",
    )(x, st, ck, cv, ng, win, cw, cb, lng, lnb, wco, sinks, wao, wout, invf, fng)


def _inv_freq_lanes():
    inv = ROPE_THETA ** (-jnp.arange(0, ROPE_DIM, 2, dtype=F32) / ROPE_DIM)
    d = jnp.arange(LANES) % HEAD_DIM
    return jnp.where(d < ROPE_DIM, inv[d % (ROPE_DIM // 2)], 0.0).astype(F32)[None, :]


def kernel(x_prompt, x_sample, state_conv, cache_k_win, cache_v_win, norm_g, w_in, conv_w, conv_b,
           conv_ln_g, conv_ln_b, w_conv_out, attn_sinks, w_attn_out, w_out, final_norm_g):
    depth = w_in.shape[0]
    n_batch = x_prompt.shape[0]
    n_dec = x_sample.shape[0]
    invf = _inv_freq_lanes()
    fng = final_norm_g[None, :]
    hp = x_prompt
    hs = x_sample[:, 0, :]
    ck = cache_k_win.reshape(depth, n_dec, WINDOW, KV_WIDTH)
    cv = cache_v_win.reshape(depth, n_dec, WINDOW, KV_WIDTH)
    outs_p, outs_s = [], []
    for l in range(depth):
        head = (norm_g[l][None, :], w_in[l].astype(BF16), conv_w[l], conv_b[l][None, :],
                conv_ln_g[l][None, :], conv_ln_b[l][None, :], w_conv_out[l].astype(BF16))
        tail = (w_attn_out[l].astype(BF16), w_out[l].astype(BF16), invf, fng)
        sinks_col = attn_sinks[l].reshape(N_TILES, 2).T.reshape(N_HEADS, 1)
        final = l == depth - 1
        hp, c1, k1, v1 = _prompt_layer(hp, *head, attn_sinks[l], *tail, final_norm=final)
        hs, c2, k2, v2 = _sample_layer(hs, state_conv, ck, cv, *head, sinks_col, *tail,
                                       layer=l, final_norm=final)
        outs_p.append((c1, k1, v1))
        outs_s.append((c2, k2, v2))
    kv_p = (depth, n_batch, WINDOW, N_KV_HEADS, HEAD_DIM)
    kv_s = (depth, n_dec, WINDOW, N_KV_HEADS, HEAD_DIM)
    return (hp, hs[:, None, :],
            jnp.stack([o[0] for o in outs_p]),
            jnp.stack([o[1] for o in outs_p]).reshape(kv_p),
            jnp.stack([o[2] for o in outs_p]).reshape(kv_p),
            jnp.stack([o[0] for o in outs_s]),
            jnp.stack([o[1] for o in outs_s]).reshape(kv_s),
            jnp.stack([o[2] for o in outs_s]).reshape(kv_s))
```

```python
import functools

import jax
import jax.numpy as jnp
from jax import lax
from jax.experimental import pallas as pl
from jax.experimental.pallas import tpu as pltpu

D_MODEL = 1024
CONV_KERNEL = 31
CONV_BUF = CONV_KERNEL - 1
N_HEADS = 16
HEAD_DIM = 64
N_KV_HEADS = 2
GROUP = N_HEADS // N_KV_HEADS
KV_WIDTH = N_KV_HEADS * HEAD_DIM
WINDOW = 128
BLOCK = 128
ROPE_DIM = HEAD_DIM // 4
ROPE_THETA = 500000.0
EPS = 1e-6
NEG = -1e30
PAST_LEN = 16384

COL_GLU_A = 0
COL_GLU_B = 1024
COL_GATE_A = 2048
COL_Q = 3072
COL_K = 4096
COL_V = 4224
COL_GATE_B = 4352
COL_MG_A = 5376
COL_MG_B = 6400
IN_COLS = 7424

LANES = 128
N_TILES = D_MODEL // LANES
CONV_PAD = 32
CONV_ROW_TILE = 32
PROMPT_T = 256
SAMPLE_NS = 8
ROPE_TABLE_ROWS = 512
VMEM_LIMIT = 56 * 1024 * 1024

F32 = jnp.float32
BF16 = jnp.bfloat16


def _dot(a, b):
    return jnp.dot(a, b, preferred_element_type=F32)


def _dot_nt(a, b):
    return lax.dot_general(a, b, (((1,), (1,)), ((), ())), preferred_element_type=F32)


def _sigmoid(x):
    return 0.5 * jnp.tanh(0.5 * x) + 0.5


def _silu(x):
    return x * _sigmoid(x)


def _rms_norm(x, g):
    ms = jnp.mean(x * x, axis=-1, keepdims=True)
    return x * lax.rsqrt(ms + EPS) * g


def _layer_norm(x, g, b):
    mu = jnp.mean(x, axis=-1, keepdims=True)
    xc = x - mu
    var = jnp.mean(xc * xc, axis=-1, keepdims=True)
    return xc * lax.rsqrt(var + EPS) * g + b


def _rope_tables(pos, invf):
    ang = pos * invf
    cos = jnp.cos(ang)
    sin = jnp.sin(ang)
    d = lax.broadcasted_iota(jnp.int32, ang.shape, 1) % HEAD_DIM
    half = ROPE_DIM // 2
    s_hi = jnp.where((d >= half) & (d < ROPE_DIM), sin, 0.0)
    s_lo = jnp.where(d < half, -sin, 0.0)
    return cos, s_hi, s_lo


def _rope_tile(x, cos, s_hi, s_lo):
    half = ROPE_DIM // 2
    return x * cos + pltpu.roll(x, half, 1) * s_hi + pltpu.roll(x, LANES - half, 1) * s_lo


def _prompt_layer_kernel(x_ref, ng_ref, win_ref, cw_ref, cb_ref, lng_ref, lnb_ref, wco_ref,
                         sinks_ref, wao_ref, wout_ref, rope_ref, fng_ref,
                         y_ref, nconv_ref, nk_ref, nv_ref,
                         h_ref, full_ref, ga_ref, cc_ref, q_ref, kd_ref, vd_ref, gb_ref, og_ref,
                         mga_ref, mgb_ref, *, T, final_norm):
    i = pl.program_id(1)
    n_qblk = T // BLOCK

    @pl.when(i == 0)
    def _():
        full_ref[:, 0:CONV_PAD, :] = jnp.zeros((N_TILES, CONV_PAD, LANES), F32)
        kd_ref[:, 0:BLOCK, :] = jnp.zeros((N_KV_HEADS, BLOCK, LANES), BF16)
        vd_ref[:, :, 0:BLOCK, :] = jnp.zeros((N_KV_HEADS, 2, BLOCK, LANES), BF16)

    h_ref[...] = _rms_norm(x_ref[...], ng_ref[...]).astype(BF16)

    def in_proj(col, width=D_MODEL):
        return _dot(h_ref[...], win_ref[:, col:col + width])

    base = CONV_PAD - CONV_BUF
    half_rows = CONV_ROW_TILE // 2

    def conv_group(r0):
        acc = [[None] * N_TILES, [None] * N_TILES]
        for c in range(N_TILES):
            lanes = slice(c * LANES, (c + 1) * LANES)
            for j in range(CONV_KERNEL):
                w = cw_ref[j:j + 1, lanes]
                for par in range(2):
                    tap = full_ref[c, pl.ds(r0 + base + j + par, half_rows, stride=2), :] * w
                    acc[par][c] = tap if j == 0 else acc[par][c] + tap
        for par in range(2):
            rows = pl.ds(r0 + par, half_rows, stride=2)
            c_rows = jnp.concatenate(acc[par], axis=1) + cb_ref[...]
            y = _layer_norm(c_rows, lng_ref[...], lnb_ref[...])
            g_rows = jnp.concatenate([ga_ref[c, rows, :] for c in range(N_TILES)], axis=1)
            cc = _silu(y) * g_rows
            for c in range(N_TILES):
                cc_ref[c, rows, :] = cc[:, c * LANES:(c + 1) * LANES]

    cos, s_hi, s_lo = rope_ref[0], rope_ref[1], rope_ref[2]
    scale = HEAD_DIM ** -0.5

    u = in_proj(COL_GLU_A) * _sigmoid(in_proj(COL_GLU_B))
    for c in range(N_TILES):
        full_ref[c, CONV_PAD:CONV_PAD + T, :] = u[:, c * LANES:(c + 1) * LANES]
    nconv_ref[...] = u[T - CONV_BUF:T, :]
    ga = _silu(in_proj(COL_GATE_A))
    for c in range(N_TILES):
        ga_ref[c] = ga[:, c * LANES:(c + 1) * LANES]

    q = in_proj(COL_Q)
    for m in range(N_TILES):
        lanes = slice(m * LANES, (m + 1) * LANES)
        q_ref[:, lanes] = (_rope_tile(q[:, lanes], cos, s_hi, s_lo) * scale).astype(BF16)
    k = _rope_tile(in_proj(COL_K, KV_WIDTH), cos, s_hi, s_lo)
    v = in_proj(COL_V, KV_WIDTH)
    nk_ref[...] = k[T - WINDOW:T, :]
    nv_ref[...] = v[T - WINDOW:T, :]
    lane = lax.broadcasted_iota(jnp.int32, (T, LANES), 1)
    lo = lane < HEAD_DIM
    k_sw = pltpu.roll(k, HEAD_DIM, 1)
    v_sw = pltpu.roll(v, HEAD_DIM, 1)
    kd_ref[0, BLOCK:BLOCK + T, :] = jnp.where(lo, k, k_sw).astype(BF16)
    kd_ref[1, BLOCK:BLOCK + T, :] = jnp.where(lo, k_sw, k).astype(BF16)
    vd_ref[0, 0, BLOCK:BLOCK + T, :] = jnp.where(lo, v, 0.0).astype(BF16)
    vd_ref[0, 1, BLOCK:BLOCK + T, :] = jnp.where(lo, 0.0, v_sw).astype(BF16)
    vd_ref[1, 0, BLOCK:BLOCK + T, :] = jnp.where(lo, v_sw, 0.0).astype(BF16)
    vd_ref[1, 1, BLOCK:BLOCK + T, :] = jnp.where(lo, 0.0, v).astype(BF16)
    gb_ref[...] = _silu(in_proj(COL_GATE_B))
    mga_ref[...] = _sigmoid(in_proj(COL_MG_A))
    mgb_ref[...] = _sigmoid(in_proj(COL_MG_B))

    qi = lax.broadcasted_iota(jnp.int32, (BLOCK, 2 * BLOCK), 0)
    kj = lax.broadcasted_iota(jnp.int32, (BLOCK, 2 * BLOCK), 1)
    band = (kj - qi >= 1) & (kj - qi <= WINDOW)
    first_key = jnp.where(i > 0, 0, BLOCK)
    band_first = band & (kj >= first_key)
    lane_b = lax.broadcasted_iota(jnp.int32, (BLOCK, LANES), 1)
    lo_b = lane_b < HEAD_DIM

    def attention_tile(a, m):
        rows = slice(a * BLOCK, (a + 1) * BLOCK)
        keys = slice(a * BLOCK, a * BLOCK + 2 * BLOCK)
        mask = band_first if a == 0 else band
        lanes = slice(m * LANES, (m + 1) * LANES)
        g = (2 * m) // GROUP
        q_tile = q_ref[rows, lanes]
        o_pair = jnp.zeros((BLOCK, LANES), F32)
        inv_pair = jnp.zeros((BLOCK, LANES), F32)
        for p in range(2):
            sink = sinks_ref[2 * m + p]
            half = lo_b if p == 0 else jnp.logical_not(lo_b)
            qm = jnp.where(half, q_tile, jnp.zeros_like(q_tile))
            s = _dot_nt(qm, kd_ref[g, keys, :])
            s = jnp.where(mask, s, NEG)
            mx = jnp.maximum(jnp.max(s, axis=-1, keepdims=True), sink)
            pr = jnp.exp(s - mx)
            den = jnp.sum(pr, axis=-1, keepdims=True) + jnp.exp(sink - mx)
            o_pair = o_pair + _dot(pr.astype(BF16), vd_ref[g, p, keys, :])
            inv_pair = jnp.where(half, 1.0 / den, inv_pair)
        og_ref[rows, lanes] = (o_pair * inv_pair * gb_ref[rows, lanes]).astype(BF16)

    for a in range(n_qblk):
        for m in range(N_TILES):
            attention_tile(a, m)

    kd_ref[:, 0:BLOCK, :] = kd_ref[:, T:T + BLOCK, :]
    vd_ref[:, :, 0:BLOCK, :] = vd_ref[:, :, T:T + BLOCK, :]

    for r0 in range(0, T, CONV_ROW_TILE):
        conv_group(r0)
    full_ref[:, 0:CONV_PAD, :] = full_ref[:, T:T + CONV_PAD, :]

    cc_all = jnp.concatenate([cc_ref[c] for c in range(N_TILES)], axis=1).astype(BF16)
    y_a = _dot(cc_all, wco_ref[...])
    y_b = _dot(og_ref[...], wao_ref[...])
    y = (mga_ref[...] * y_a + mgb_ref[...] * y_b).astype(BF16)
    out = x_ref[...] + _dot(y, wout_ref[...])
    if final_norm:
        out = _rms_norm(out, fng_ref[...])
    y_ref[...] = out


def _full_spec(shape):
    nd = len(shape)
    return pl.BlockSpec(shape, lambda *_: (0,) * nd)


def _prompt_layer(x, ng, win, cw, cb, lng, lnb, wco, sinks, wao, wout, rope, fng, *, final_norm):
    B, S, _ = x.shape
    T = PROMPT_T
    kernel = functools.partial(_prompt_layer_kernel, T=T, final_norm=final_norm)
    row = (1, D_MODEL)
    in_specs = [
        pl.BlockSpec((None, T, D_MODEL), lambda b, i: (b, i, 0)),
        _full_spec(row),
        _full_spec((D_MODEL, IN_COLS)),
        _full_spec((CONV_KERNEL, D_MODEL)),
        _full_spec(row), _full_spec(row), _full_spec(row),
        _full_spec((D_MODEL, D_MODEL)),
        pl.BlockSpec(memory_space=pltpu.SMEM),
        _full_spec((D_MODEL, D_MODEL)),
        _full_spec((D_MODEL, D_MODEL)),
        pl.BlockSpec((3, T, LANES), lambda b, i: (0, i, 0)),
        _full_spec(row),
    ]
    out_specs = [
        pl.BlockSpec((None, T, D_MODEL), lambda b, i: (b, i, 0)),
        pl.BlockSpec((None, CONV_BUF, D_MODEL), lambda b, i: (b, 0, 0)),
        pl.BlockSpec((None, WINDOW, KV_WIDTH), lambda b, i: (b, 0, 0)),
        pl.BlockSpec((None, WINDOW, KV_WIDTH), lambda b, i: (b, 0, 0)),
    ]
    out_shape = [
        jax.ShapeDtypeStruct((B, S, D_MODEL), F32),
        jax.ShapeDtypeStruct((B, CONV_BUF, D_MODEL), F32),
        jax.ShapeDtypeStruct((B, WINDOW, KV_WIDTH), F32),
        jax.ShapeDtypeStruct((B, WINDOW, KV_WIDTH), F32),
    ]
    scratch = [
        pltpu.VMEM((T, D_MODEL), BF16),
        pltpu.VMEM((N_TILES, CONV_PAD + T, LANES), F32),
        pltpu.VMEM((N_TILES, T, LANES), F32),
        pltpu.VMEM((N_TILES, T, LANES), F32),
        pltpu.VMEM((T, D_MODEL), BF16),
        pltpu.VMEM((N_KV_HEADS, BLOCK + T, LANES), BF16),
        pltpu.VMEM((N_KV_HEADS, 2, BLOCK + T, LANES), BF16),
        pltpu.VMEM((T, D_MODEL), F32),
        pltpu.VMEM((T, D_MODEL), BF16),
        pltpu.VMEM((T, D_MODEL), F32),
        pltpu.VMEM((T, D_MODEL), F32),
    ]
    return pl.pallas_call(
        kernel,
        grid=(B, S // T),
        in_specs=in_specs,
        out_specs=out_specs,
        out_shape=out_shape,
        scratch_shapes=scratch,
        compiler_params=pltpu.CompilerParams(
            dimension_semantics=("arbitrary", "arbitrary"),
            vmem_limit_bytes=VMEM_LIMIT),
        name="prompt_layer",
    )(x, ng, win, cw, cb, lng, lnb, wco, sinks, wao, wout, rope, fng)


def _rope_table_kernel(invf_ref, out_ref, *, R):
    pos = (pl.program_id(0) * R + lax.broadcasted_iota(jnp.int32, (R, 1), 0)).astype(F32)
    for t, table in enumerate(_rope_tables(pos, invf_ref[...])):
        out_ref[t] = table


def _rope_table(invf, seq):
    R = ROPE_TABLE_ROWS
    return pl.pallas_call(
        functools.partial(_rope_table_kernel, R=R),
        grid=(seq // R,),
        in_specs=[_full_spec((1, LANES))],
        out_specs=pl.BlockSpec((3, R, LANES), lambda r: (0, r, 0)),
        out_shape=jax.ShapeDtypeStruct((3, seq, LANES), F32),
        name="rope_table",
    )(invf)


def _decode_kernel(x_ref, st_ref, ck_ref, cv_ref, ng_ref, win_ref, cw_ref, cb_ref, lng_ref,
                   lnb_ref, wco_ref, sinks_ref, wao_ref, wout_ref, invf_ref, fng_ref,
                   y_ref, nconv_ref, nk_ref, nv_ref,
                   xcur_ref, h_ref, u_ref, q_ref, kt_ref, vt_ref, c_ref, o_ref, qt_ref, ot_ref,
                   oblk_ref, *, NS):
    l = pl.program_id(0)
    s = pl.program_id(1)
    last_layer = pl.num_programs(0) - 1
    last = pl.num_programs(1) - 1
    N = x_ref.shape[0]

    @pl.when(s == 0)
    def _():
        @pl.when(l == 0)
        def _():
            xcur_ref[...] = x_ref[...]

        h = _rms_norm(xcur_ref[...], ng_ref[...]).astype(BF16)
        h_ref[...] = h
        za = _dot(h, win_ref[:, COL_GLU_A:COL_GLU_A + D_MODEL])
        zb = _dot(h, win_ref[:, COL_GLU_B:COL_GLU_B + D_MODEL])
        u_ref[...] = za * _sigmoid(zb)
        pos = jnp.full((N, 1), float(PAST_LEN), F32)
        cos, s_hi, s_lo = _rope_tables(pos, invf_ref[...])
        scale = HEAD_DIM ** -0.5
        q = _dot(h, win_ref[:, COL_Q:COL_Q + D_MODEL])
        for m in range(N_TILES):
            lanes = slice(m * LANES, (m + 1) * LANES)
            q_ref[:, lanes] = _rope_tile(q[:, lanes], cos, s_hi, s_lo) * scale
        k = _rope_tile(_dot(h, win_ref[:, COL_K:COL_K + KV_WIDTH]), cos, s_hi, s_lo)
        v = _dot(h, win_ref[:, COL_V:COL_V + KV_WIDTH])
        kt_ref[...] = k.T
        vt_ref[...] = v.T

    sink_col = sinks_ref[...]
    row8 = lax.broadcasted_iota(jnp.int32, (N_TILES, LANES), 0)
    lane8 = lax.broadcasted_iota(jnp.int32, (N_TILES, LANES), 1)
    lo8 = lane8 < HEAD_DIM
    g0 = row8 < (N_TILES // 2)
    lane_w = lax.broadcasted_iota(jnp.int32, (KV_WIDTH, WINDOW), 1)
    newest = lane_w == WINDOW - 1

    blk = pl.ds(pl.multiple_of(s * NS, NS), NS)
    u_blk = u_ref[blk, :]
    q_blk = q_ref[blk, :]

    acc = u_blk * cw_ref[CONV_BUF:CONV_KERNEL, :]
    for j in range(CONV_BUF):
        acc = acc + st_ref[j] * cw_ref[j:j + 1, :]
    c_ref[blk, :] = acc + cb_ref[...]
    for j in range(CONV_BUF - 1):
        nconv_ref[j] = st_ref[j + 1]
    nconv_ref[CONV_BUF - 1] = u_blk

    for n in range(NS):
        mine = lane_w == s * NS + n
        k_col = jnp.sum(jnp.where(mine, kt_ref[...], 0.0), axis=1, keepdims=True)
        v_col = jnp.sum(jnp.where(mine, vt_ref[...], 0.0), axis=1, keepdims=True)
        kt = jnp.where(newest, k_col, pltpu.roll(ck_ref[n], WINDOW - 1, 1))
        vt = jnp.where(newest, v_col, pltpu.roll(cv_ref[n], WINDOW - 1, 1))
        nk_ref[n] = kt
        nv_ref[n] = vt
        for m in range(N_TILES):
            qt_ref[m:m + 1, :] = q_blk[n:n + 1, m * LANES:(m + 1) * LANES]
        qt = qt_ref[...]
        qe = jnp.where(lo8, qt, 0.0)
        qo = jnp.where(lo8, 0.0, qt)
        q_even = jnp.where(g0, qe, pltpu.roll(qe, HEAD_DIM, 1))
        q_odd = jnp.where(g0, pltpu.roll(qo, HEAD_DIM, 1), qo)
        lhs = jnp.concatenate([q_even, q_odd], axis=0).astype(BF16)
        sc = _dot(lhs, kt.astype(BF16))
        mx = jnp.maximum(jnp.max(sc, axis=-1, keepdims=True), sink_col)
        pr = jnp.exp(sc - mx)
        den = jnp.sum(pr, axis=-1, keepdims=True) + jnp.exp(sink_col - mx)
        o = _dot_nt(pr.astype(BF16), vt.astype(BF16)) * (1.0 / den)
        o_even = o[0:N_TILES, :]
        o_odd = o[N_TILES:N_HEADS, :]
        oe = jnp.where(g0, o_even, pltpu.roll(o_even, HEAD_DIM, 1))
        oo = jnp.where(g0, pltpu.roll(o_odd, HEAD_DIM, 1), o_odd)
        ot_ref[...] = jnp.where(lo8, oe, oo)
        for m in range(N_TILES):
            oblk_ref[n:n + 1, m * LANES:(m + 1) * LANES] = ot_ref[m:m + 1, :]
    o_ref[blk, :] = oblk_ref[...]

    @pl.when(s == last)
    def _():
        h = h_ref[...]
        ga = _silu(_dot(h, win_ref[:, COL_GATE_A:COL_GATE_A + D_MODEL]))
        cc = _silu(_layer_norm(c_ref[...], lng_ref[...], lnb_ref[...])) * ga
        y_a = _dot(cc.astype(BF16), wco_ref[...])
        gb = _silu(_dot(h, win_ref[:, COL_GATE_B:COL_GATE_B + D_MODEL]))
        y_b = _dot((o_ref[...] * gb).astype(BF16), wao_ref[...])
        mg_a = _sigmoid(_dot(h, win_ref[:, COL_MG_A:COL_MG_A + D_MODEL]))
        mg_b = _sigmoid(_dot(h, win_ref[:, COL_MG_B:COL_MG_B + D_MODEL]))
        y = (mg_a * y_a + mg_b * y_b).astype(BF16)
        out = xcur_ref[...] + _dot(y, wout_ref[...])
        xcur_ref[...] = out

        @pl.when(l == last_layer)
        def _():
            y_ref[...] = _rms_norm(out, fng_ref[...])


def _decode(x, st, ck, cv, ng, win, cw, cb, lng, lnb, wco, sinks, wao, wout, invf, fng):
    N = x.shape[0]
    depth = win.shape[0]
    NS = SAMPLE_NS

    def per_layer(*shape):
        nd = len(shape)
        return pl.BlockSpec((None,) + shape, lambda l, s: (l,) + (0,) * nd)

    def weight(*shape):
        nd = len(shape)
        return pl.BlockSpec((None,) + shape, lambda l, s: (l,) + (0,) * nd,
                            pipeline_mode=pl.Buffered(1))

    in_specs = [
        pl.BlockSpec((N, D_MODEL), lambda l, s: (0, 0)),
        pl.BlockSpec((None, CONV_BUF, NS, D_MODEL), lambda l, s: (l, 0, s, 0)),
        pl.BlockSpec((None, NS, KV_WIDTH, WINDOW), lambda l, s: (l, s, 0, 0)),
        pl.BlockSpec((None, NS, KV_WIDTH, WINDOW), lambda l, s: (l, s, 0, 0)),
        per_layer(1, D_MODEL),
        weight(D_MODEL, IN_COLS),
        per_layer(CONV_KERNEL, D_MODEL),
        per_layer(1, D_MODEL), per_layer(1, D_MODEL), per_layer(1, D_MODEL),
        weight(D_MODEL, D_MODEL),
        per_layer(N_HEADS, 1),
        weight(D_MODEL, D_MODEL),
        weight(D_MODEL, D_MODEL),
        pl.BlockSpec((1, LANES), lambda l, s: (0, 0)),
        pl.BlockSpec((1, D_MODEL), lambda l, s: (0, 0)),
    ]
    out_specs = [
        pl.BlockSpec((N, D_MODEL), lambda l, s: (0, 0)),
        pl.BlockSpec((None, CONV_BUF, NS, D_MODEL), lambda l, s: (l, 0, s, 0)),
        pl.BlockSpec((None, NS, KV_WIDTH, WINDOW), lambda l, s: (l, s, 0, 0)),
        pl.BlockSpec((None, NS, KV_WIDTH, WINDOW), lambda l, s: (l, s, 0, 0)),
    ]
    out_shape = [
        jax.ShapeDtypeStruct((N, D_MODEL), F32),
        jax.ShapeDtypeStruct((depth, CONV_BUF, N, D_MODEL), F32),
        jax.ShapeDtypeStruct((depth, N, KV_WIDTH, WINDOW), F32),
        jax.ShapeDtypeStruct((depth, N, KV_WIDTH, WINDOW), F32),
    ]
    scratch = [
        pltpu.VMEM((N, D_MODEL), F32),
        pltpu.VMEM((N, D_MODEL), BF16),
        pltpu.VMEM((N, D_MODEL), F32),
        pltpu.VMEM((N, D_MODEL), F32),
        pltpu.VMEM((KV_WIDTH, N), F32),
        pltpu.VMEM((KV_WIDTH, N), F32),
        pltpu.VMEM((N, D_MODEL), F32),
        pltpu.VMEM((N, D_MODEL), F32),
        pltpu.VMEM((N_TILES, LANES), F32),
        pltpu.VMEM((N_TILES, LANES), F32),
        pltpu.VMEM((NS, D_MODEL), F32),
    ]
    return pl.pallas_call(
        functools.partial(_decode_kernel, NS=NS),
        grid=(depth, N // NS),
        in_specs=in_specs,
        out_specs=out_specs,
        out_shape=out_shape,
        scratch_shapes=scratch,
        compiler_params=pltpu.CompilerParams(
            dimension_semantics=("arbitrary", "arbitrary"),
            vmem_limit_bytes=VMEM_LIMIT),
        name="decode_layers",
    )(x, st, ck, cv, ng, win, cw, cb, lng, lnb, wco, sinks, wao, wout, invf, fng)


def _inv_freq_lanes():
    inv = ROPE_THETA ** (-jnp.arange(0, ROPE_DIM, 2, dtype=F32) / ROPE_DIM)
    d = jnp.arange(LANES) % HEAD_DIM
    return jnp.where(d < ROPE_DIM, inv[d % (ROPE_DIM // 2)], 0.0).astype(F32)[None, :]


def kernel(x_prompt, x_sample, state_conv, cache_k_win, cache_v_win, norm_g, w_in, conv_w, conv_b,
           conv_ln_g, conv_ln_b, w_conv_out, attn_sinks, w_attn_out, w_out, final_norm_g):
    depth = w_in.shape[0]
    n_batch, seq = x_prompt.shape[0], x_prompt.shape[1]
    n_dec = x_sample.shape[0]
    invf = _inv_freq_lanes()
    fng = final_norm_g[None, :]
    win, wco, wao, wout = (w.astype(BF16) for w in (w_in, w_conv_out, w_attn_out, w_out))
    row = lambda p: p[:, None, :]

    st = jnp.transpose(state_conv, (0, 2, 1, 3))
    ck = jnp.transpose(cache_k_win, (0, 1, 3, 4, 2)).reshape(depth, n_dec, KV_WIDTH, WINDOW)
    cv = jnp.transpose(cache_v_win, (0, 1, 3, 4, 2)).reshape(depth, n_dec, KV_WIDTH, WINDOW)
    sinks_col = jnp.swapaxes(attn_sinks.reshape(depth, N_TILES, 2), 1, 2).reshape(depth, N_HEADS, 1)
    ys, nconv_s, nk_s, nv_s = _decode(
        x_sample[:, 0, :], st, ck, cv, row(norm_g), win, conv_w, row(conv_b), row(conv_ln_g),
        row(conv_ln_b), wco, sinks_col, wao, wout, invf, fng)
    kv_t = (depth, n_dec, N_KV_HEADS, HEAD_DIM, WINDOW)
    new_conv_sample = jnp.transpose(nconv_s, (0, 2, 1, 3))
    new_k_sample = jnp.transpose(nk_s.reshape(kv_t), (0, 1, 4, 2, 3))
    new_v_sample = jnp.transpose(nv_s.reshape(kv_t), (0, 1, 4, 2, 3))

    rope = _rope_table(invf, seq)
    hp = x_prompt
    outs_p = []
    for l in range(depth):
        hp, c1, k1, v1 = _prompt_layer(
            hp, norm_g[l][None, :], win[l], conv_w[l], conv_b[l][None, :], conv_ln_g[l][None, :],
            conv_ln_b[l][None, :], wco[l], attn_sinks[l], wao[l], wout[l], rope, fng,
            final_norm=l == depth - 1)
        outs_p.append((c1, k1, v1))
    kv_p = (depth, n_batch, WINDOW, N_KV_HEADS, HEAD_DIM)
    return (hp, ys[:, None, :],
            jnp.stack([o[0] for o in outs_p]),
            jnp.stack([o[1] for o in outs_p]).reshape(kv_p),
            jnp.stack([o[2] for o in outs_p]).reshape(kv_p),
            new_conv_sample, new_k_sample, new_v_sample)
```

```python
import functools

import jax
import jax.numpy as jnp
from jax import lax
from jax.experimental import pallas as pl
from jax.experimental.pallas import tpu as pltpu

D_MODEL = 1024
CONV_KERNEL = 31
CONV_BUF = CONV_KERNEL - 1
N_HEADS = 16
HEAD_DIM = 64
N_KV_HEADS = 2
GROUP = N_HEADS // N_KV_HEADS
KV_WIDTH = N_KV_HEADS * HEAD_DIM
WINDOW = 128
BLOCK = 128
ROPE_DIM = HEAD_DIM // 4
ROPE_THETA = 500000.0
EPS = 1e-6
NEG = -1e30
PAST_LEN = 16384

COL_GLU_A = 0
COL_GLU_B = 1024
COL_GATE_A = 2048
COL_Q = 3072
COL_K = 4096
COL_V = 4224
COL_GATE_B = 4352
COL_MG_A = 5376
COL_MG_B = 6400
IN_COLS = 7424

LANES = 128
N_TILES = D_MODEL // LANES
CONV_PAD = 32
CONV_ROW_TILE = 32
PROMPT_T = 256
SAMPLE_NS = 8
ROPE_TABLE_ROWS = 512
VMEM_LIMIT = 56 * 1024 * 1024

F32 = jnp.float32
BF16 = jnp.bfloat16


def _dot(a, b):
    return jnp.dot(a, b, preferred_element_type=F32)


def _dot_nt(a, b):
    return lax.dot_general(a, b, (((1,), (1,)), ((), ())), preferred_element_type=F32)


def _sigmoid(x):
    return 0.5 * jnp.tanh(0.5 * x) + 0.5


def _silu(x):
    return x * _sigmoid(x)


def _rms_norm(x, g):
    ms = jnp.mean(x * x, axis=-1, keepdims=True)
    return x * lax.rsqrt(ms + EPS) * g


def _layer_norm(x, g, b):
    mu = jnp.mean(x, axis=-1, keepdims=True)
    xc = x - mu
    var = jnp.mean(xc * xc, axis=-1, keepdims=True)
    return xc * lax.rsqrt(var + EPS) * g + b


def _rope_tables(pos, invf):
    ang = pos * invf
    cos = jnp.cos(ang)
    sin = jnp.sin(ang)
    d = lax.broadcasted_iota(jnp.int32, ang.shape, 1) % HEAD_DIM
    half = ROPE_DIM // 2
    s_hi = jnp.where((d >= half) & (d < ROPE_DIM), sin, 0.0)
    s_lo = jnp.where(d < half, -sin, 0.0)
    return cos, s_hi, s_lo


def _rope_tile(x, cos, s_hi, s_lo):
    half = ROPE_DIM // 2
    return x * cos + pltpu.roll(x, half, 1) * s_hi + pltpu.roll(x, LANES - half, 1) * s_lo


def _prompt_layer_kernel(x_ref, ng_ref, win_ref, cw_ref, cb_ref, lng_ref, lnb_ref, wco_ref,
                         sinks_ref, wao_ref, wout_ref, rope_ref, fng_ref,
                         y_ref, nconv_ref, nk_ref, nv_ref,
                         h_ref, full_ref, ga_ref, cc_ref, q_ref, kd_ref, vd_ref, gb_ref, og_ref,
                         mga_ref, mgb_ref, *, T, final_norm):
    i = pl.program_id(1)
    n_qblk = T // BLOCK

    @pl.when(i == 0)
    def _():
        full_ref[:, 0:CONV_PAD, :] = jnp.zeros((N_TILES, CONV_PAD, LANES), F32)
        kd_ref[:, 0:BLOCK, :] = jnp.zeros((N_KV_HEADS, BLOCK, LANES), BF16)
        vd_ref[:, :, 0:BLOCK, :] = jnp.zeros((N_KV_HEADS, 2, BLOCK, LANES), BF16)

    h_ref[...] = _rms_norm(x_ref[...], ng_ref[...]).astype(BF16)

    def in_proj(col, width=D_MODEL):
        return _dot(h_ref[...], win_ref[:, col:col + width])

    base = CONV_PAD - CONV_BUF
    half_rows = CONV_ROW_TILE // 2

    def conv_group(r0):
        acc = [[None] * N_TILES, [None] * N_TILES]
        for c in range(N_TILES):
            lanes = slice(c * LANES, (c + 1) * LANES)
            for j in range(CONV_KERNEL):
                w = cw_ref[j:j + 1, lanes]
                for par in range(2):
                    tap = full_ref[c, pl.ds(r0 + base + j + par, half_rows, stride=2), :] * w
                    acc[par][c] = tap if j == 0 else acc[par][c] + tap
        for par in range(2):
            rows = pl.ds(r0 + par, half_rows, stride=2)
            c_rows = jnp.concatenate(acc[par], axis=1) + cb_ref[...]
            y = _layer_norm(c_rows, lng_ref[...], lnb_ref[...])
            g_rows = jnp.concatenate([ga_ref[c, rows, :] for c in range(N_TILES)], axis=1)
            cc = _silu(y) * g_rows
            for c in range(N_TILES):
                cc_ref[c, rows, :] = cc[:, c * LANES:(c + 1) * LANES]

    cos, s_hi, s_lo = rope_ref[0], rope_ref[1], rope_ref[2]
    scale = HEAD_DIM ** -0.5

    u = in_proj(COL_GLU_A) * _sigmoid(in_proj(COL_GLU_B))
    for c in range(N_TILES):
        full_ref[c, CONV_PAD:CONV_PAD + T, :] = u[:, c * LANES:(c + 1) * LANES]
    nconv_ref[...] = u[T - CONV_BUF:T, :]
    ga = _silu(in_proj(COL_GATE_A))
    for c in range(N_TILES):
        ga_ref[c] = ga[:, c * LANES:(c + 1) * LANES]

    q = in_proj(COL_Q)
    for m in range(N_TILES):
        lanes = slice(m * LANES, (m + 1) * LANES)
        q_ref[:, lanes] = (_rope_tile(q[:, lanes], cos, s_hi, s_lo) * scale).astype(BF16)
    k = _rope_tile(in_proj(COL_K, KV_WIDTH), cos, s_hi, s_lo)
    v = in_proj(COL_V, KV_WIDTH)
    nk_ref[...] = k[T - WINDOW:T, :]
    nv_ref[...] = v[T - WINDOW:T, :]
    lane = lax.broadcasted_iota(jnp.int32, (T, LANES), 1)
    lo = lane < HEAD_DIM
    k_sw = pltpu.roll(k, HEAD_DIM, 1)
    v_sw = pltpu.roll(v, HEAD_DIM, 1)
    kd_ref[0, BLOCK:BLOCK + T, :] = jnp.where(lo, k, k_sw).astype(BF16)
    kd_ref[1, BLOCK:BLOCK + T, :] = jnp.where(lo, k_sw, k).astype(BF16)
    vd_ref[0, 0, BLOCK:BLOCK + T, :] = jnp.where(lo, v, 0.0).astype(BF16)
    vd_ref[0, 1, BLOCK:BLOCK + T, :] = jnp.where(lo, 0.0, v_sw).astype(BF16)
    vd_ref[1, 0, BLOCK:BLOCK + T, :] = jnp.where(lo, v_sw, 0.0).astype(BF16)
    vd_ref[1, 1, BLOCK:BLOCK + T, :] = jnp.where(lo, 0.0, v).astype(BF16)
    gb_ref[...] = _silu(in_proj(COL_GATE_B))
    mga_ref[...] = _sigmoid(in_proj(COL_MG_A))
    mgb_ref[...] = _sigmoid(in_proj(COL_MG_B))

    qi = lax.broadcasted_iota(jnp.int32, (BLOCK, 2 * BLOCK), 0)
    kj = lax.broadcasted_iota(jnp.int32, (BLOCK, 2 * BLOCK), 1)
    band = (kj - qi >= 1) & (kj - qi <= WINDOW)
    first_key = jnp.where(i > 0, 0, BLOCK)
    band_first = band & (kj >= first_key)
    lane_b = lax.broadcasted_iota(jnp.int32, (BLOCK, LANES), 1)
    lo_b = lane_b < HEAD_DIM

    def attention_block(a):
        rows = slice(a * BLOCK, (a + 1) * BLOCK)
        keys = slice(a * BLOCK, a * BLOCK + 2 * BLOCK)
        mask = band_first if a == 0 else band
        scores = []
        for m in range(N_TILES):
            g = (2 * m) // GROUP
            q_tile = q_ref[rows, m * LANES:(m + 1) * LANES]
            for p in range(2):
                half = lo_b if p == 0 else jnp.logical_not(lo_b)
                qm = jnp.where(half, q_tile, jnp.zeros_like(q_tile))
                scores.append(_dot_nt(qm, kd_ref[g, keys, :]))
        probs, invs = [], []
        for head, s in enumerate(scores):
            sink = sinks_ref[head]
            s = jnp.where(mask, s, NEG)
            mx = jnp.maximum(jnp.max(s, axis=-1, keepdims=True), sink)
            pr = jnp.exp(s - mx)
            den = jnp.sum(pr, axis=-1, keepdims=True) + jnp.exp(sink - mx)
            probs.append(pr.astype(BF16))
            invs.append(1.0 / den)
        for m in range(N_TILES):
            g = (2 * m) // GROUP
            lanes = slice(m * LANES, (m + 1) * LANES)
            o_pair = (_dot(probs[2 * m], vd_ref[g, 0, keys, :])
                      + _dot(probs[2 * m + 1], vd_ref[g, 1, keys, :]))
            inv_pair = jnp.where(lo_b, invs[2 * m], invs[2 * m + 1])
            og_ref[rows, lanes] = (o_pair * inv_pair * gb_ref[rows, lanes]).astype(BF16)

    for a in range(n_qblk):
        attention_block(a)

    kd_ref[:, 0:BLOCK, :] = kd_ref[:, T:T + BLOCK, :]
    vd_ref[:, :, 0:BLOCK, :] = vd_ref[:, :, T:T + BLOCK, :]

    for r0 in range(0, T, CONV_ROW_TILE):
        conv_group(r0)
    full_ref[:, 0:CONV_PAD, :] = full_ref[:, T:T + CONV_PAD, :]

    cc_all = jnp.concatenate([cc_ref[c] for c in range(N_TILES)], axis=1).astype(BF16)
    y_a = _dot(cc_all, wco_ref[...])
    y_b = _dot(og_ref[...], wao_ref[...])
    y = (mga_ref[...] * y_a + mgb_ref[...] * y_b).astype(BF16)
    out = x_ref[...] + _dot(y, wout_ref[...])
    if final_norm:
        out = _rms_norm(out, fng_ref[...])
    y_ref[...] = out


def _full_spec(shape):
    nd = len(shape)
    return pl.BlockSpec(shape, lambda *_: (0,) * nd)


def _prompt_layer(x, ng, win, cw, cb, lng, lnb, wco, sinks, wao, wout, rope, fng, *, final_norm):
    B, S, _ = x.shape
    T = PROMPT_T
    kernel = functools.partial(_prompt_layer_kernel, T=T, final_norm=final_norm)
    row = (1, D_MODEL)
    in_specs = [
        pl.BlockSpec((None, T, D_MODEL), lambda b, i: (b, i, 0)),
        _full_spec(row),
        _full_spec((D_MODEL, IN_COLS)),
        _full_spec((CONV_KERNEL, D_MODEL)),
        _full_spec(row), _full_spec(row), _full_spec(row),
        _full_spec((D_MODEL, D_MODEL)),
        pl.BlockSpec(memory_space=pltpu.SMEM),
        _full_spec((D_MODEL, D_MODEL)),
        _full_spec((D_MODEL, D_MODEL)),
        pl.BlockSpec((3, T, LANES), lambda b, i: (0, i, 0)),
        _full_spec(row),
    ]
    out_specs = [
        pl.BlockSpec((None, T, D_MODEL), lambda b, i: (b, i, 0)),
        pl.BlockSpec((None, CONV_BUF, D_MODEL), lambda b, i: (b, 0, 0)),
        pl.BlockSpec((None, WINDOW, KV_WIDTH), lambda b, i: (b, 0, 0)),
        pl.BlockSpec((None, WINDOW, KV_WIDTH), lambda b, i: (b, 0, 0)),
    ]
    out_shape = [
        jax.ShapeDtypeStruct((B, S, D_MODEL), F32),
        jax.ShapeDtypeStruct((B, CONV_BUF, D_MODEL), F32),
        jax.ShapeDtypeStruct((B, WINDOW, KV_WIDTH), F32),
        jax.ShapeDtypeStruct((B, WINDOW, KV_WIDTH), F32),
    ]
    scratch = [
        pltpu.VMEM((T, D_MODEL), BF16),
        pltpu.VMEM((N_TILES, CONV_PAD + T, LANES), F32),
        pltpu.VMEM((N_TILES, T, LANES), F32),
        pltpu.VMEM((N_TILES, T, LANES), F32),
        pltpu.VMEM((T, D_MODEL), BF16),
        pltpu.VMEM((N_KV_HEADS, BLOCK + T, LANES), BF16),
        pltpu.VMEM((N_KV_HEADS, 2, BLOCK + T, LANES), BF16),
        pltpu.VMEM((T, D_MODEL), F32),
        pltpu.VMEM((T, D_MODEL), BF16),
        pltpu.VMEM((T, D_MODEL), F32),
        pltpu.VMEM((T, D_MODEL), F32),
    ]
    return pl.pallas_call(
        kernel,
        grid=(B, S // T),
        in_specs=in_specs,
        out_specs=out_specs,
        out_shape=out_shape,
        scratch_shapes=scratch,
        compiler_params=pltpu.CompilerParams(
            dimension_semantics=("arbitrary", "arbitrary"),
            vmem_limit_bytes=VMEM_LIMIT),
        name="prompt_layer",
    )(x, ng, win, cw, cb, lng, lnb, wco, sinks, wao, wout, rope, fng)


def _rope_table_kernel(invf_ref, out_ref, *, R):
    pos = (pl.program_id(0) * R + lax.broadcasted_iota(jnp.int32, (R, 1), 0)).astype(F32)
    for t, table in enumerate(_rope_tables(pos, invf_ref[...])):
        out_ref[t] = table


def _rope_table(invf, seq):
    R = ROPE_TABLE_ROWS
    return pl.pallas_call(
        functools.partial(_rope_table_kernel, R=R),
        grid=(seq // R,),
        in_specs=[_full_spec((1, LANES))],
        out_specs=pl.BlockSpec((3, R, LANES), lambda r: (0, r, 0)),
        out_shape=jax.ShapeDtypeStruct((3, seq, LANES), F32),
        name="rope_table",
    )(invf)


def _decode_kernel(x_ref, st_ref, ck_ref, cv_ref, ng_ref, win_ref, cw_ref, cb_ref, lng_ref,
                   lnb_ref, wco_ref, sinks_ref, wao_ref, wout_ref, invf_ref, fng_ref,
                   y_ref, nconv_ref, nk_ref, nv_ref,
                   xcur_ref, h_ref, u_ref, q_ref, kt_ref, vt_ref, c_ref, o_ref, qt_ref, ot_ref,
                   oblk_ref, *, NS):
    l = pl.program_id(0)
    s = pl.program_id(1)
    last_layer = pl.num_programs(0) - 1
    last = pl.num_programs(1) - 1
    N = x_ref.shape[0]

    @pl.when(s == 0)
    def _():
        @pl.when(l == 0)
        def _():
            xcur_ref[...] = x_ref[...]

        h = _rms_norm(xcur_ref[...], ng_ref[...]).astype(BF16)
        h_ref[...] = h
        za = _dot(h, win_ref[:, COL_GLU_A:COL_GLU_A + D_MODEL])
        zb = _dot(h, win_ref[:, COL_GLU_B:COL_GLU_B + D_MODEL])
        u_ref[...] = za * _sigmoid(zb)
        pos = jnp.full((N, 1), float(PAST_LEN), F32)
        cos, s_hi, s_lo = _rope_tables(pos, invf_ref[...])
        scale = HEAD_DIM ** -0.5
        q = _dot(h, win_ref[:, COL_Q:COL_Q + D_MODEL])
        for m in range(N_TILES):
            lanes = slice(m * LANES, (m + 1) * LANES)
            q_ref[:, lanes] = _rope_tile(q[:, lanes], cos, s_hi, s_lo) * scale
        k = _rope_tile(_dot(h, win_ref[:, COL_K:COL_K + KV_WIDTH]), cos, s_hi, s_lo)
        v = _dot(h, win_ref[:, COL_V:COL_V + KV_WIDTH])
        kt_ref[...] = k.T
        vt_ref[...] = v.T

    sink_col = sinks_ref[...]
    row8 = lax.broadcasted_iota(jnp.int32, (N_TILES, LANES), 0)
    lane8 = lax.broadcasted_iota(jnp.int32, (N_TILES, LANES), 1)
    lo8 = lane8 < HEAD_DIM
    g0 = row8 < (N_TILES // 2)
    lane_w = lax.broadcasted_iota(jnp.int32, (KV_WIDTH, WINDOW), 1)
    newest = lane_w == WINDOW - 1

    blk = pl.ds(pl.multiple_of(s * NS, NS), NS)
    u_blk = u_ref[blk, :]
    q_blk = q_ref[blk, :]

    acc = u_blk * cw_ref[CONV_BUF:CONV_KERNEL, :]
    for j in range(CONV_BUF):
        acc = acc + st_ref[j] * cw_ref[j:j + 1, :]
    c_ref[blk, :] = acc + cb_ref[...]
    for j in range(CONV_BUF - 1):
        nconv_ref[j] = st_ref[j + 1]
    nconv_ref[CONV_BUF - 1] = u_blk

    samples = range(NS)
    mine = [lane_w == s * NS + n for n in samples]
    k_col = [jnp.sum(jnp.where(mine[n], kt_ref[...], 0.0), axis=1, keepdims=True) for n in samples]
    v_col = [jnp.sum(jnp.where(mine[n], vt_ref[...], 0.0), axis=1, keepdims=True) for n in samples]
    kt = [jnp.where(newest, k_col[n], pltpu.roll(ck_ref[n], WINDOW - 1, 1)) for n in samples]
    vt = [jnp.where(newest, v_col[n], pltpu.roll(cv_ref[n], WINDOW - 1, 1)) for n in samples]
    for n in samples:
        nk_ref[n] = kt[n]
        nv_ref[n] = vt[n]
    for n in samples:
        for m in range(N_TILES):
            qt_ref[n, m:m + 1, :] = q_blk[n:n + 1, m * LANES:(m + 1) * LANES]
    lhs = []
    for n in samples:
        qt = qt_ref[n]
        qe = jnp.where(lo8, qt, 0.0)
        qo = jnp.where(lo8, 0.0, qt)
        q_even = jnp.where(g0, qe, pltpu.roll(qe, HEAD_DIM, 1))
        q_odd = jnp.where(g0, pltpu.roll(qo, HEAD_DIM, 1), qo)
        lhs.append(jnp.concatenate([q_even, q_odd], axis=0).astype(BF16))
    sc = [_dot(lhs[n], kt[n].astype(BF16)) for n in samples]
    pr, inv = [], []
    for n in samples:
        mx = jnp.maximum(jnp.max(sc[n], axis=-1, keepdims=True), sink_col)
        p = jnp.exp(sc[n] - mx)
        inv.append(1.0 / (jnp.sum(p, axis=-1, keepdims=True) + jnp.exp(sink_col - mx)))
        pr.append(p.astype(BF16))
    o = [_dot_nt(pr[n], vt[n].astype(BF16)) * inv[n] for n in samples]
    for n in samples:
        o_even = o[n][0:N_TILES, :]
        o_odd = o[n][N_TILES:N_HEADS, :]
        oe = jnp.where(g0, o_even, pltpu.roll(o_even, HEAD_DIM, 1))
        oo = jnp.where(g0, pltpu.roll(o_odd, HEAD_DIM, 1), o_odd)
        ot_ref[n] = jnp.where(lo8, oe, oo)
    for n in samples:
        for m in range(N_TILES):
            oblk_ref[n:n + 1, m * LANES:(m + 1) * LANES] = ot_ref[n, m:m + 1, :]
    o_ref[blk, :] = oblk_ref[...]

    @pl.when(s == last)
    def _():
        h = h_ref[...]
        ga = _silu(_dot(h, win_ref[:, COL_GATE_A:COL_GATE_A + D_MODEL]))
        cc = _silu(_layer_norm(c_ref[...], lng_ref[...], lnb_ref[...])) * ga
        y_a = _dot(cc.astype(BF16), wco_ref[...])
        gb = _silu(_dot(h, win_ref[:, COL_GATE_B:COL_GATE_B + D_MODEL]))
        y_b = _dot((o_ref[...] * gb).astype(BF16), wao_ref[...])
        mg_a = _sigmoid(_dot(h, win_ref[:, COL_MG_A:COL_MG_A + D_MODEL]))
        mg_b = _sigmoid(_dot(h, win_ref[:, COL_MG_B:COL_MG_B + D_MODEL]))
        y = (mg_a * y_a + mg_b * y_b).astype(BF16)
        out = xcur_ref[...] + _dot(y, wout_ref[...])
        xcur_ref[...] = out

        @pl.when(l == last_layer)
        def _():
            y_ref[...] = _rms_norm(out, fng_ref[...])


def _decode(x, st, ck, cv, ng, win, cw, cb, lng, lnb, wco, sinks, wao, wout, invf, fng):
    N = x.shape[0]
    depth = win.shape[0]
    NS = SAMPLE_NS

    def per_layer(*shape):
        nd = len(shape)
        return pl.BlockSpec((None,) + shape, lambda l, s: (l,) + (0,) * nd)

    def weight(*shape):
        nd = len(shape)
        return pl.BlockSpec((None,) + shape, lambda l, s: (l,) + (0,) * nd,
                            pipeline_mode=pl.Buffered(1))

    in_specs = [
        pl.BlockSpec((N, D_MODEL), lambda l, s: (0, 0)),
        pl.BlockSpec((None, CONV_BUF, NS, D_MODEL), lambda l, s: (l, 0, s, 0)),
        pl.BlockSpec((None, NS, KV_WIDTH, WINDOW), lambda l, s: (l, s, 0, 0)),
        pl.BlockSpec((None, NS, KV_WIDTH, WINDOW), lambda l, s: (l, s, 0, 0)),
        per_layer(1, D_MODEL),
        weight(D_MODEL, IN_COLS),
        per_layer(CONV_KERNEL, D_MODEL),
        per_layer(1, D_MODEL), per_layer(1, D_MODEL), per_layer(1, D_MODEL),
        weight(D_MODEL, D_MODEL),
        per_layer(N_HEADS, 1),
        weight(D_MODEL, D_MODEL),
        weight(D_MODEL, D_MODEL),
        pl.BlockSpec((1, LANES), lambda l, s: (0, 0)),
        pl.BlockSpec((1, D_MODEL), lambda l, s: (0, 0)),
    ]
    out_specs = [
        pl.BlockSpec((N, D_MODEL), lambda l, s: (0, 0)),
        pl.BlockSpec((None, CONV_BUF, NS, D_MODEL), lambda l, s: (l, 0, s, 0)),
        pl.BlockSpec((None, NS, KV_WIDTH, WINDOW), lambda l, s: (l, s, 0, 0)),
        pl.BlockSpec((None, NS, KV_WIDTH, WINDOW), lambda l, s: (l, s, 0, 0)),
    ]
    out_shape = [
        jax.ShapeDtypeStruct((N, D_MODEL), F32),
        jax.ShapeDtypeStruct((depth, CONV_BUF, N, D_MODEL), F32),
        jax.ShapeDtypeStruct((depth, N, KV_WIDTH, WINDOW), F32),
        jax.ShapeDtypeStruct((depth, N, KV_WIDTH, WINDOW), F32),
    ]
    scratch = [
        pltpu.VMEM((N, D_MODEL), F32),
        pltpu.VMEM((N, D_MODEL), BF16),
        pltpu.VMEM((N, D_MODEL), F32),
        pltpu.VMEM((N, D_MODEL), F32),
        pltpu.VMEM((KV_WIDTH, N), F32),
        pltpu.VMEM((KV_WIDTH, N), F32),
        pltpu.VMEM((N, D_MODEL), F32),
        pltpu.VMEM((N, D_MODEL), F32),
        pltpu.VMEM((NS, N_TILES, LANES), F32),
        pltpu.VMEM((NS, N_TILES, LANES), F32),
        pltpu.VMEM((NS, D_MODEL), F32),
    ]
    return pl.pallas_call(
        functools.partial(_decode_kernel, NS=NS),
        grid=(depth, N // NS),
        in_specs=in_specs,
        out_specs=out_specs,
        out_shape=out_shape,
        scratch_shapes=scratch,
        compiler_params=pltpu.CompilerParams(
            dimension_semantics=("arbitrary", "arbitrary"),
            vmem_limit_bytes=VMEM_LIMIT),
        name="decode_layers",
    )(x, st, ck, cv, ng, win, cw, cb, lng, lnb, wco, sinks, wao, wout, invf, fng)


def _inv_freq_lanes():
    inv = ROPE_THETA ** (-jnp.arange(0, ROPE_DIM, 2, dtype=F32) / ROPE_DIM)
    d = jnp.arange(LANES) % HEAD_DIM
    return jnp.where(d < ROPE_DIM, inv[d % (ROPE_DIM // 2)], 0.0).astype(F32)[None, :]


def kernel(x_prompt, x_sample, state_conv, cache_k_win, cache_v_win, norm_g, w_in, conv_w, conv_b,
           conv_ln_g, conv_ln_b, w_conv_out, attn_sinks, w_attn_out, w_out, final_norm_g):
    depth = w_in.shape[0]
    n_batch, seq = x_prompt.shape[0], x_prompt.shape[1]
    n_dec = x_sample.shape[0]
    invf = _inv_freq_lanes()
    fng = final_norm_g[None, :]
    win, wco, wao, wout = (w.astype(BF16) for w in (w_in, w_conv_out, w_attn_out, w_out))
    row = lambda p: p[:, None, :]

    st = jnp.transpose(state_conv, (0, 2, 1, 3))
    ck = jnp.transpose(cache_k_win, (0, 1, 3, 4, 2)).reshape(depth, n_dec, KV_WIDTH, WINDOW)
    cv = jnp.transpose(cache_v_win, (0, 1, 3, 4, 2)).reshape(depth, n_dec, KV_WIDTH, WINDOW)
    sinks_col = jnp.swapaxes(attn_sinks.reshape(depth, N_TILES, 2), 1, 2).reshape(depth, N_HEADS, 1)
    ys, nconv_s, nk_s, nv_s = _decode(
        x_sample[:, 0, :], st, ck, cv, row(norm_g), win, conv_w, row(conv_b), row(conv_ln_g),
        row(conv_ln_b), wco, sinks_col, wao, wout, invf, fng)
    kv_t = (depth, n_dec, N_KV_HEADS, HEAD_DIM, WINDOW)
    new_conv_sample = jnp.transpose(nconv_s, (0, 2, 1, 3))
    new_k_sample = jnp.transpose(nk_s.reshape(kv_t), (0, 1, 4, 2, 3))
    new_v_sample = jnp.transpose(nv_s.reshape(kv_t), (0, 1, 4, 2, 3))

    rope = _rope_table(invf, seq)
    hp = x_prompt
    outs_p = []
    for l in range(depth):
        hp, c1, k1, v1 = _prompt_layer(
            hp, norm_g[l][None, :], win[l], conv_w[l], conv_b[l][None, :], conv_ln_g[l][None, :],
            conv_ln_b[l][None, :], wco[l], attn_sinks[l], wao[l], wout[l], rope, fng,
            final_norm=l == depth - 1)
        outs_p.append((c1, k1, v1))
    kv_p = (depth, n_batch, WINDOW, N_KV_HEADS, HEAD_DIM)
    return (hp, ys[:, None, :],
            jnp.stack([o[0] for o in outs_p]),
            jnp.stack([o[1] for o in outs_p]).reshape(kv_p),
            jnp.stack([o[2] for o in outs_p]).reshape(kv_p),
            new_conv_sample, new_k_sample, new_v_sample)
```

```python
import functools

import jax
import jax.numpy as jnp
from jax import lax
from jax.experimental import pallas as pl
from jax.experimental.pallas import tpu as pltpu

D_MODEL = 1024
CONV_KERNEL = 31
CONV_BUF = CONV_KERNEL - 1
N_HEADS = 16
HEAD_DIM = 64
N_KV_HEADS = 2
GROUP = N_HEADS // N_KV_HEADS
KV_WIDTH = N_KV_HEADS * HEAD_DIM
WINDOW = 128
BLOCK = 128
ROPE_DIM = HEAD_DIM // 4
ROPE_THETA = 500000.0
EPS = 1e-6
NEG = -1e30
PAST_LEN = 16384

COL_GLU_A = 0
COL_GLU_B = 1024
COL_GATE_A = 2048
COL_Q = 3072
COL_K = 4096
COL_V = 4224
COL_GATE_B = 4352
COL_MG_A = 5376
COL_MG_B = 6400
IN_COLS = 7424

LANES = 128
N_TILES = D_MODEL // LANES
CONV_PAD = 32
CONV_ROW_TILE = 32
PROMPT_T = 256
SAMPLE_NS = 8
ROPE_TABLE_ROWS = 512
LOG2E = 1.4426950408889634
VMEM_LIMIT = 56 * 1024 * 1024

F32 = jnp.float32
BF16 = jnp.bfloat16


def _dot(a, b):
    return jnp.dot(a, b, preferred_element_type=F32)


def _dot_nt(a, b):
    return lax.dot_general(a, b, (((1,), (1,)), ((), ())), preferred_element_type=F32)


def _sigmoid(x):
    return 0.5 * jnp.tanh(0.5 * x) + 0.5


def _silu(x):
    hx = 0.5 * x
    return hx + hx * jnp.tanh(hx)


def _rms_norm(x, g):
    ms = jnp.mean(x * x, axis=-1, keepdims=True)
    return x * lax.rsqrt(ms + EPS) * g


def _layer_norm(x, g, b):
    mu = jnp.mean(x, axis=-1, keepdims=True)
    xc = x - mu
    var = jnp.mean(xc * xc, axis=-1, keepdims=True)
    return xc * lax.rsqrt(var + EPS) * g + b


def _rope_tables(pos, invf):
    ang = pos * invf
    cos = jnp.cos(ang)
    sin = jnp.sin(ang)
    d = lax.broadcasted_iota(jnp.int32, ang.shape, 1) % HEAD_DIM
    half = ROPE_DIM // 2
    s_hi = jnp.where((d >= half) & (d < ROPE_DIM), sin, 0.0)
    s_lo = jnp.where(d < half, -sin, 0.0)
    return cos, s_hi, s_lo


def _rope_tile(x, cos, s_hi, s_lo):
    half = ROPE_DIM // 2
    return x * cos + pltpu.roll(x, half, 1) * s_hi + pltpu.roll(x, LANES - half, 1) * s_lo


def _prompt_layer_kernel(x_ref, ng_ref, win_ref, cw_ref, cb_ref, lng_ref, lnb_ref, wco_ref,
                         sinks_ref, wao_ref, wout_ref, rope_ref, fng_ref,
                         y_ref, nconv_ref, nk_ref, nv_ref,
                         h_ref, full_ref, ga_ref, cc_ref, q_ref, kd_ref, vd_ref, gb_ref, og_ref,
                         mga_ref, mgb_ref, *, T, final_norm):
    i = pl.program_id(1)
    n_qblk = T // BLOCK

    @pl.when(i == 0)
    def _():
        full_ref[:, 0:CONV_PAD, :] = jnp.zeros((N_TILES, CONV_PAD, LANES), F32)
        kd_ref[:, 0:BLOCK, :] = jnp.zeros((N_KV_HEADS, BLOCK, LANES), BF16)
        vd_ref[:, :, 0:BLOCK, :] = jnp.zeros((N_KV_HEADS, 2, BLOCK, LANES), BF16)

    h_ref[...] = _rms_norm(x_ref[...], ng_ref[...]).astype(BF16)

    def in_proj(col, width=D_MODEL):
        return _dot(h_ref[...], win_ref[:, col:col + width])

    base = CONV_PAD - CONV_BUF
    half_rows = CONV_ROW_TILE // 2

    def conv_group(r0):
        acc = [[None] * N_TILES, [None] * N_TILES]
        for c in range(N_TILES):
            lanes = slice(c * LANES, (c + 1) * LANES)
            for j in range(CONV_KERNEL):
                w = cw_ref[j:j + 1, lanes]
                for par in range(2):
                    tap = full_ref[c, pl.ds(r0 + base + j + par, half_rows, stride=2), :] * w
                    acc[par][c] = tap if j == 0 else acc[par][c] + tap
        for par in range(2):
            rows = pl.ds(r0 + par, half_rows, stride=2)
            c_rows = jnp.concatenate(acc[par], axis=1) + cb_ref[...]
            y = _layer_norm(c_rows, lng_ref[...], lnb_ref[...])
            g_rows = jnp.concatenate([ga_ref[c, rows, :] for c in range(N_TILES)], axis=1)
            cc = _silu(y) * g_rows
            for c in range(N_TILES):
                cc_ref[c, rows, :] = cc[:, c * LANES:(c + 1) * LANES]

    cos, s_hi, s_lo = rope_ref[0], rope_ref[1], rope_ref[2]
    scale = HEAD_DIM ** -0.5 * LOG2E

    u = in_proj(COL_GLU_A) * _sigmoid(in_proj(COL_GLU_B))
    for c in range(N_TILES):
        full_ref[c, CONV_PAD:CONV_PAD + T, :] = u[:, c * LANES:(c + 1) * LANES]
    nconv_ref[...] = u[T - CONV_BUF:T, :]
    ga = _silu(in_proj(COL_GATE_A))
    for c in range(N_TILES):
        ga_ref[c] = ga[:, c * LANES:(c + 1) * LANES]

    q = in_proj(COL_Q)
    for m in range(N_TILES):
        lanes = slice(m * LANES, (m + 1) * LANES)
        q_ref[:, lanes] = (_rope_tile(q[:, lanes], cos, s_hi, s_lo) * scale).astype(BF16)
    k = _rope_tile(in_proj(COL_K, KV_WIDTH), cos, s_hi, s_lo)
    v = in_proj(COL_V, KV_WIDTH)
    nk_ref[...] = k[T - WINDOW:T, :]
    nv_ref[...] = v[T - WINDOW:T, :]
    lane = lax.broadcasted_iota(jnp.int32, (T, LANES), 1)
    lo = lane < HEAD_DIM
    k_sw = pltpu.roll(k, HEAD_DIM, 1)
    v_sw = pltpu.roll(v, HEAD_DIM, 1)
    kd_ref[0, BLOCK:BLOCK + T, :] = jnp.where(lo, k, k_sw).astype(BF16)
    kd_ref[1, BLOCK:BLOCK + T, :] = jnp.where(lo, k_sw, k).astype(BF16)
    vd_ref[0, 0, BLOCK:BLOCK + T, :] = jnp.where(lo, v, 0.0).astype(BF16)
    vd_ref[0, 1, BLOCK:BLOCK + T, :] = jnp.where(lo, 0.0, v_sw).astype(BF16)
    vd_ref[1, 0, BLOCK:BLOCK + T, :] = jnp.where(lo, v_sw, 0.0).astype(BF16)
    vd_ref[1, 1, BLOCK:BLOCK + T, :] = jnp.where(lo, 0.0, v).astype(BF16)
    gb_ref[...] = _silu(in_proj(COL_GATE_B))
    mga_ref[...] = _sigmoid(in_proj(COL_MG_A))
    mgb_ref[...] = _sigmoid(in_proj(COL_MG_B))

    qi = lax.broadcasted_iota(jnp.int32, (BLOCK, 2 * BLOCK), 0)
    kj = lax.broadcasted_iota(jnp.int32, (BLOCK, 2 * BLOCK), 1)
    band = (kj - qi >= 1) & (kj - qi <= WINDOW)
    first_key = jnp.where(i > 0, 0, BLOCK)
    band_first = band & (kj >= first_key)
    lane_b = lax.broadcasted_iota(jnp.int32, (BLOCK, LANES), 1)
    lo_b = lane_b < HEAD_DIM

    def attention_block(a):
        rows = slice(a * BLOCK, (a + 1) * BLOCK)
        keys = slice(a * BLOCK, a * BLOCK + 2 * BLOCK)
        mask = band_first if a == 0 else band
        scores = []
        for m in range(N_TILES):
            g = (2 * m) // GROUP
            q_tile = q_ref[rows, m * LANES:(m + 1) * LANES]
            for p in range(2):
                half = lo_b if p == 0 else jnp.logical_not(lo_b)
                qm = jnp.where(half, q_tile, jnp.zeros_like(q_tile))
                scores.append(_dot_nt(qm, kd_ref[g, keys, :]))
        probs, invs = [], []
        for head, s in enumerate(scores):
            sink = sinks_ref[head] * LOG2E
            s = jnp.where(mask, s, NEG)
            mx = jnp.maximum(jnp.max(s, axis=-1, keepdims=True), sink)
            pr = jnp.exp2(s - mx)
            den = jnp.sum(pr, axis=-1, keepdims=True) + jnp.exp2(sink - mx)
            probs.append(pr.astype(BF16))
            invs.append(1.0 / den)
        for m in range(N_TILES):
            g = (2 * m) // GROUP
            lanes = slice(m * LANES, (m + 1) * LANES)
            o_pair = (_dot(probs[2 * m], vd_ref[g, 0, keys, :])
                      + _dot(probs[2 * m + 1], vd_ref[g, 1, keys, :]))
            inv_pair = jnp.where(lo_b, invs[2 * m], invs[2 * m + 1])
            og_ref[rows, lanes] = (o_pair * inv_pair * gb_ref[rows, lanes]).astype(BF16)

    for a in range(n_qblk):
        attention_block(a)

    kd_ref[:, 0:BLOCK, :] = kd_ref[:, T:T + BLOCK, :]
    vd_ref[:, :, 0:BLOCK, :] = vd_ref[:, :, T:T + BLOCK, :]

    for r0 in range(0, T, CONV_ROW_TILE):
        conv_group(r0)
    full_ref[:, 0:CONV_PAD, :] = full_ref[:, T:T + CONV_PAD, :]

    cc_all = jnp.concatenate([cc_ref[c] for c in range(N_TILES)], axis=1).astype(BF16)
    y_a = _dot(cc_all, wco_ref[...])
    y_b = _dot(og_ref[...], wao_ref[...])
    y = (mga_ref[...] * y_a + mgb_ref[...] * y_b).astype(BF16)
    out = x_ref[...] + _dot(y, wout_ref[...])
    if final_norm:
        out = _rms_norm(out, fng_ref[...])
    y_ref[...] = out


def _full_spec(shape):
    nd = len(shape)
    return pl.BlockSpec(shape, lambda *_: (0,) * nd)


def _prompt_layer(x, ng, win, cw, cb, lng, lnb, wco, sinks, wao, wout, rope, fng,
                  *, layer, final_norm):
    B, S, _ = x.shape
    T = PROMPT_T
    kernel = functools.partial(_prompt_layer_kernel, T=T, final_norm=final_norm)
    row = (1, D_MODEL)

    def weight(*shape):
        nd = len(shape)
        return pl.BlockSpec((None,) + shape, lambda b, i: (layer,) + (0,) * nd,
                            pipeline_mode=pl.Buffered(1))

    in_specs = [
        pl.BlockSpec((None, T, D_MODEL), lambda b, i: (b, i, 0)),
        _full_spec(row),
        weight(D_MODEL, IN_COLS),
        _full_spec((CONV_KERNEL, D_MODEL)),
        _full_spec(row), _full_spec(row), _full_spec(row),
        weight(D_MODEL, D_MODEL),
        pl.BlockSpec(memory_space=pltpu.SMEM),
        weight(D_MODEL, D_MODEL),
        weight(D_MODEL, D_MODEL),
        pl.BlockSpec((3, T, LANES), lambda b, i: (0, i, 0)),
        _full_spec(row),
    ]
    out_specs = [
        pl.BlockSpec((None, T, D_MODEL), lambda b, i: (b, i, 0)),
        pl.BlockSpec((None, CONV_BUF, D_MODEL), lambda b, i: (b, 0, 0)),
        pl.BlockSpec((None, WINDOW, KV_WIDTH), lambda b, i: (b, 0, 0)),
        pl.BlockSpec((None, WINDOW, KV_WIDTH), lambda b, i: (b, 0, 0)),
    ]
    out_shape = [
        jax.ShapeDtypeStruct((B, S, D_MODEL), F32),
        jax.ShapeDtypeStruct((B, CONV_BUF, D_MODEL), F32),
        jax.ShapeDtypeStruct((B, WINDOW, KV_WIDTH), F32),
        jax.ShapeDtypeStruct((B, WINDOW, KV_WIDTH), F32),
    ]
    scratch = [
        pltpu.VMEM((T, D_MODEL), BF16),
        pltpu.VMEM((N_TILES, CONV_PAD + T, LANES), F32),
        pltpu.VMEM((N_TILES, T, LANES), F32),
        pltpu.VMEM((N_TILES, T, LANES), F32),
        pltpu.VMEM((T, D_MODEL), BF16),
        pltpu.VMEM((N_KV_HEADS, BLOCK + T, LANES), BF16),
        pltpu.VMEM((N_KV_HEADS, 2, BLOCK + T, LANES), BF16),
        pltpu.VMEM((T, D_MODEL), F32),
        pltpu.VMEM((T, D_MODEL), BF16),
        pltpu.VMEM((T, D_MODEL), F32),
        pltpu.VMEM((T, D_MODEL), F32),
    ]
    return pl.pallas_call(
        kernel,
        grid=(B, S // T),
        in_specs=in_specs,
        out_specs=out_specs,
        out_shape=out_shape,
        scratch_shapes=scratch,
        compiler_params=pltpu.CompilerParams(
            dimension_semantics=("arbitrary", "arbitrary"),
            vmem_limit_bytes=VMEM_LIMIT),
        name="prompt_layer",
    )(x, ng, win, cw, cb, lng, lnb, wco, sinks, wao, wout, rope, fng)


def _rope_table_kernel(invf_ref, out_ref, *, R):
    pos = (pl.program_id(0) * R + lax.broadcasted_iota(jnp.int32, (R, 1), 0)).astype(F32)
    for t, table in enumerate(_rope_tables(pos, invf_ref[...])):
        out_ref[t] = table


def _rope_table(invf, seq):
    R = ROPE_TABLE_ROWS
    return pl.pallas_call(
        functools.partial(_rope_table_kernel, R=R),
        grid=(seq // R,),
        in_specs=[_full_spec((1, LANES))],
        out_specs=pl.BlockSpec((3, R, LANES), lambda r: (0, r, 0)),
        out_shape=jax.ShapeDtypeStruct((3, seq, LANES), F32),
        name="rope_table",
    )(invf)


def _decode_kernel(x_ref, st_ref, ck_ref, cv_ref, ng_ref, win_ref, cw_ref, cb_ref, lng_ref,
                   lnb_ref, wco_ref, sinks_ref, wao_ref, wout_ref, invf_ref, fng_ref,
                   y_ref, nconv_ref, nk_ref, nv_ref,
                   xcur_ref, h_ref, u_ref, q_ref, kt_ref, vt_ref, c_ref, o_ref, qt_ref, ot_ref,
                   oblk_ref, *, NS):
    l = pl.program_id(0)
    s = pl.program_id(1)
    last_layer = pl.num_programs(0) - 1
    last = pl.num_programs(1) - 1
    N = x_ref.shape[0]

    @pl.when(s == 0)
    def _():
        @pl.when(l == 0)
        def _():
            xcur_ref[...] = x_ref[...]

        h = _rms_norm(xcur_ref[...], ng_ref[...]).astype(BF16)
        h_ref[...] = h
        za = _dot(h, win_ref[:, COL_GLU_A:COL_GLU_A + D_MODEL])
        zb = _dot(h, win_ref[:, COL_GLU_B:COL_GLU_B + D_MODEL])
        u_ref[...] = za * _sigmoid(zb)
        pos = jnp.full((N, 1), float(PAST_LEN), F32)
        cos, s_hi, s_lo = _rope_tables(pos, invf_ref[...])
        scale = HEAD_DIM ** -0.5
        q = _dot(h, win_ref[:, COL_Q:COL_Q + D_MODEL])
        for m in range(N_TILES):
            lanes = slice(m * LANES, (m + 1) * LANES)
            q_ref[:, lanes] = _rope_tile(q[:, lanes], cos, s_hi, s_lo) * scale
        k = _rope_tile(_dot(h, win_ref[:, COL_K:COL_K + KV_WIDTH]), cos, s_hi, s_lo)
        v = _dot(h, win_ref[:, COL_V:COL_V + KV_WIDTH])
        kt_ref[...] = k.T
        vt_ref[...] = v.T

    sink_col = sinks_ref[...]
    row8 = lax.broadcasted_iota(jnp.int32, (N_TILES, LANES), 0)
    lane8 = lax.broadcasted_iota(jnp.int32, (N_TILES, LANES), 1)
    lo8 = lane8 < HEAD_DIM
    g0 = row8 < (N_TILES // 2)
    lane_w = lax.broadcasted_iota(jnp.int32, (KV_WIDTH, WINDOW), 1)
    newest = lane_w == WINDOW - 1

    blk = pl.ds(pl.multiple_of(s * NS, NS), NS)
    u_blk = u_ref[blk, :]
    q_blk = q_ref[blk, :]

    acc = u_blk * cw_ref[CONV_BUF:CONV_KERNEL, :]
    for j in range(CONV_BUF):
        acc = acc + st_ref[j] * cw_ref[j:j + 1, :]
    c_ref[blk, :] = acc + cb_ref[...]
    for j in range(CONV_BUF - 1):
        nconv_ref[j] = st_ref[j + 1]
    nconv_ref[CONV_BUF - 1] = u_blk

    samples = range(NS)
    mine = [lane_w == s * NS + n for n in samples]
    k_col = [jnp.sum(jnp.where(mine[n], kt_ref[...], 0.0), axis=1, keepdims=True) for n in samples]
    v_col = [jnp.sum(jnp.where(mine[n], vt_ref[...], 0.0), axis=1, keepdims=True) for n in samples]
    kt = [jnp.where(newest, k_col[n], pltpu.roll(ck_ref[n], WINDOW - 1, 1)) for n in samples]
    vt = [jnp.where(newest, v_col[n], pltpu.roll(cv_ref[n], WINDOW - 1, 1)) for n in samples]
    for n in samples:
        nk_ref[n] = kt[n]
        nv_ref[n] = vt[n]
    for n in samples:
        for m in range(N_TILES):
            qt_ref[n, m:m + 1, :] = q_blk[n:n + 1, m * LANES:(m + 1) * LANES]
    lhs = []
    for n in samples:
        qt = qt_ref[n]
        qe = jnp.where(lo8, qt, 0.0)
        qo = jnp.where(lo8, 0.0, qt)
        q_even = jnp.where(g0, qe, pltpu.roll(qe, HEAD_DIM, 1))
        q_odd = jnp.where(g0, pltpu.roll(qo, HEAD_DIM, 1), qo)
        lhs.append(jnp.concatenate([q_even, q_odd], axis=0).astype(BF16))
    sc = [_dot(lhs[n], kt[n].astype(BF16)) for n in samples]
    pr, inv = [], []
    for n in samples:
        mx = jnp.maximum(jnp.max(sc[n], axis=-1, keepdims=True), sink_col)
        p = jnp.exp(sc[n] - mx)
        inv.append(1.0 / (jnp.sum(p, axis=-1, keepdims=True) + jnp.exp(sink_col - mx)))
        pr.append(p.astype(BF16))
    o = [_dot_nt(pr[n], vt[n].astype(BF16)) * inv[n] for n in samples]
    for n in samples:
        o_even = o[n][0:N_TILES, :]
        o_odd = o[n][N_TILES:N_HEADS, :]
        oe = jnp.where(g0, o_even, pltpu.roll(o_even, HEAD_DIM, 1))
        oo = jnp.where(g0, pltpu.roll(o_odd, HEAD_DIM, 1), o_odd)
        ot_ref[n] = jnp.where(lo8, oe, oo)
    for n in samples:
        for m in range(N_TILES):
            oblk_ref[n:n + 1, m * LANES:(m + 1) * LANES] = ot_ref[n, m:m + 1, :]
    o_ref[blk, :] = oblk_ref[...]

    @pl.when(s == last)
    def _():
        h = h_ref[...]
        ga = _silu(_dot(h, win_ref[:, COL_GATE_A:COL_GATE_A + D_MODEL]))
        cc = _silu(_layer_norm(c_ref[...], lng_ref[...], lnb_ref[...])) * ga
        y_a = _dot(cc.astype(BF16), wco_ref[...])
        gb = _silu(_dot(h, win_ref[:, COL_GATE_B:COL_GATE_B + D_MODEL]))
        y_b = _dot((o_ref[...] * gb).astype(BF16), wao_ref[...])
        mg_a = _sigmoid(_dot(h, win_ref[:, COL_MG_A:COL_MG_A + D_MODEL]))
        mg_b = _sigmoid(_dot(h, win_ref[:, COL_MG_B:COL_MG_B + D_MODEL]))
        y = (mg_a * y_a + mg_b * y_b).astype(BF16)
        out = xcur_ref[...] + _dot(y, wout_ref[...])
        xcur_ref[...] = out

        @pl.when(l == last_layer)
        def _():
            y_ref[...] = _rms_norm(out, fng_ref[...])


def _decode(x, st, ck, cv, ng, win, cw, cb, lng, lnb, wco, sinks, wao, wout, invf, fng):
    N = x.shape[0]
    depth = win.shape[0]
    NS = SAMPLE_NS

    def per_layer(*shape):
        nd = len(shape)
        return pl.BlockSpec((None,) + shape, lambda l, s: (l,) + (0,) * nd)

    def weight(*shape):
        nd = len(shape)
        return pl.BlockSpec((None,) + shape, lambda l, s: (l,) + (0,) * nd,
                            pipeline_mode=pl.Buffered(1))

    in_specs = [
        pl.BlockSpec((N, D_MODEL), lambda l, s: (0, 0)),
        pl.BlockSpec((None, CONV_BUF, NS, D_MODEL), lambda l, s: (l, 0, s, 0)),
        pl.BlockSpec((None, NS, KV_WIDTH, WINDOW), lambda l, s: (l, s, 0, 0)),
        pl.BlockSpec((None, NS, KV_WIDTH, WINDOW), lambda l, s: (l, s, 0, 0)),
        per_layer(1, D_MODEL),
        weight(D_MODEL, IN_COLS),
        per_layer(CONV_KERNEL, D_MODEL),
        per_layer(1, D_MODEL), per_layer(1, D_MODEL), per_layer(1, D_MODEL),
        weight(D_MODEL, D_MODEL),
        per_layer(N_HEADS, 1),
        weight(D_MODEL, D_MODEL),
        weight(D_MODEL, D_MODEL),
        pl.BlockSpec((1, LANES), lambda l, s: (0, 0)),
        pl.BlockSpec((1, D_MODEL), lambda l, s: (0, 0)),
    ]
    out_specs = [
        pl.BlockSpec((N, D_MODEL), lambda l, s: (0, 0)),
        pl.BlockSpec((None, CONV_BUF, NS, D_MODEL), lambda l, s: (l, 0, s, 0)),
        pl.BlockSpec((None, NS, KV_WIDTH, WINDOW), lambda l, s: (l, s, 0, 0)),
        pl.BlockSpec((None, NS, KV_WIDTH, WINDOW), lambda l, s: (l, s, 0, 0)),
    ]
    out_shape = [
        jax.ShapeDtypeStruct((N, D_MODEL), F32),
        jax.ShapeDtypeStruct((depth, CONV_BUF, N, D_MODEL), F32),
        jax.ShapeDtypeStruct((depth, N, KV_WIDTH, WINDOW), F32),
        jax.ShapeDtypeStruct((depth, N, KV_WIDTH, WINDOW), F32),
    ]
    scratch = [
        pltpu.VMEM((N, D_MODEL), F32),
        pltpu.VMEM((N, D_MODEL), BF16),
        pltpu.VMEM((N, D_MODEL), F32),
        pltpu.VMEM((N, D_MODEL), F32),
        pltpu.VMEM((KV_WIDTH, N), F32),
        pltpu.VMEM((KV_WIDTH, N), F32),
        pltpu.VMEM((N, D_MODEL), F32),
        pltpu.VMEM((N, D_MODEL), F32),
        pltpu.VMEM((NS, N_TILES, LANES), F32),
        pltpu.VMEM((NS, N_TILES, LANES), F32),
        pltpu.VMEM((NS, D_MODEL), F32),
    ]
    return pl.pallas_call(
        functools.partial(_decode_kernel, NS=NS),
        grid=(depth, N // NS),
        in_specs=in_specs,
        out_specs=out_specs,
        out_shape=out_shape,
        scratch_shapes=scratch,
        compiler_params=pltpu.CompilerParams(
            dimension_semantics=("arbitrary", "arbitrary"),
            vmem_limit_bytes=VMEM_LIMIT),
        name="decode_layers",
    )(x, st, ck, cv, ng, win, cw, cb, lng, lnb, wco, sinks, wao, wout, invf, fng)


def _inv_freq_lanes():
    inv = ROPE_THETA ** (-jnp.arange(0, ROPE_DIM, 2, dtype=F32) / ROPE_DIM)
    d = jnp.arange(LANES) % HEAD_DIM
    return jnp.where(d < ROPE_DIM, inv[d % (ROPE_DIM // 2)], 0.0).astype(F32)[None, :]


def kernel(x_prompt, x_sample, state_conv, cache_k_win, cache_v_win, norm_g, w_in, conv_w, conv_b,
           conv_ln_g, conv_ln_b, w_conv_out, attn_sinks, w_attn_out, w_out, final_norm_g):
    depth = w_in.shape[0]
    n_batch, seq = x_prompt.shape[0], x_prompt.shape[1]
    n_dec = x_sample.shape[0]
    invf = _inv_freq_lanes()
    fng = final_norm_g[None, :]
    win, wco, wao, wout = (w.astype(BF16) for w in (w_in, w_conv_out, w_attn_out, w_out))
    row = lambda p: p[:, None, :]

    st = jnp.transpose(state_conv, (0, 2, 1, 3))
    ck = jnp.transpose(cache_k_win, (0, 1, 3, 4, 2)).reshape(depth, n_dec, KV_WIDTH, WINDOW)
    cv = jnp.transpose(cache_v_win, (0, 1, 3, 4, 2)).reshape(depth, n_dec, KV_WIDTH, WINDOW)
    sinks_col = jnp.swapaxes(attn_sinks.reshape(depth, N_TILES, 2), 1, 2).reshape(depth, N_HEADS, 1)
    ys, nconv_s, nk_s, nv_s = _decode(
        x_sample[:, 0, :], st, ck, cv, row(norm_g), win, conv_w, row(conv_b), row(conv_ln_g),
        row(conv_ln_b), wco, sinks_col, wao, wout, invf, fng)
    kv_t = (depth, n_dec, N_KV_HEADS, HEAD_DIM, WINDOW)
    new_conv_sample = jnp.transpose(nconv_s, (0, 2, 1, 3))
    new_k_sample = jnp.transpose(nk_s.reshape(kv_t), (0, 1, 4, 2, 3))
    new_v_sample = jnp.transpose(nv_s.reshape(kv_t), (0, 1, 4, 2, 3))

    rope = _rope_table(invf, seq)
    hp = x_prompt
    outs_p = []
    for l in range(depth):
        hp, c1, k1, v1 = _prompt_layer(
            hp, norm_g[l][None, :], win, conv_w[l], conv_b[l][None, :], conv_ln_g[l][None, :],
            conv_ln_b[l][None, :], wco, attn_sinks[l], wao, wout, rope, fng,
            layer=l, final_norm=l == depth - 1)
        outs_p.append((c1, k1, v1))
    kv_p = (depth, n_batch, WINDOW, N_KV_HEADS, HEAD_DIM)
    return (hp, ys[:, None, :],
            jnp.stack([o[0] for o in outs_p]),
            jnp.stack([o[1] for o in outs_p]).reshape(kv_p),
            jnp.stack([o[2] for o in outs_p]).reshape(kv_p),
            new_conv_sample, new_k_sample, new_v_sample)
```

```python
import functools

import jax
import jax.numpy as jnp
from jax import lax
from jax.experimental import pallas as pl
from jax.experimental.pallas import tpu as pltpu

D_MODEL = 1024
CONV_KERNEL = 31
CONV_BUF = CONV_KERNEL - 1
N_HEADS = 16
HEAD_DIM = 64
N_KV_HEADS = 2
GROUP = N_HEADS // N_KV_HEADS
KV_WIDTH = N_KV_HEADS * HEAD_DIM
WINDOW = 128
BLOCK = 128
ROPE_DIM = HEAD_DIM // 4
ROPE_THETA = 500000.0
EPS = 1e-6
NEG = -1e30
PAST_LEN = 16384

COL_GLU_A = 0
COL_GLU_B = 1024
COL_GATE_A = 2048
COL_Q = 3072
COL_K = 4096
COL_V = 4224
COL_GATE_B = 4352
COL_MG_A = 5376
COL_MG_B = 6400
IN_COLS = 7424

LANES = 128
N_TILES = D_MODEL // LANES
CONV_PAD = 32
CONV_ROW_TILE = 32
PROMPT_T = 256
SAMPLE_NS = 8
ROPE_TABLE_ROWS = 512
PROJ_CHUNK = 512
LOG2E = 1.4426950408889634
VMEM_LIMIT = 56 * 1024 * 1024

F32 = jnp.float32
BF16 = jnp.bfloat16


def _dot(a, b):
    return jnp.dot(a, b, preferred_element_type=F32)


def _dot_nt(a, b):
    return lax.dot_general(a, b, (((1,), (1,)), ((), ())), preferred_element_type=F32)


def _sigmoid(x):
    return 0.5 * jnp.tanh(0.5 * x) + 0.5


def _silu(x):
    hx = 0.5 * x
    return hx + hx * jnp.tanh(hx)


def _rms_norm(x, g):
    ms = jnp.mean(x * x, axis=-1, keepdims=True)
    return x * lax.rsqrt(ms + EPS) * g


def _layer_norm(x, g, b):
    mu = jnp.mean(x, axis=-1, keepdims=True)
    xc = x - mu
    var = jnp.mean(xc * xc, axis=-1, keepdims=True)
    return xc * lax.rsqrt(var + EPS) * g + b


def _rope_tables(pos, invf):
    ang = pos * invf
    cos = jnp.cos(ang)
    sin = jnp.sin(ang)
    d = lax.broadcasted_iota(jnp.int32, ang.shape, 1) % HEAD_DIM
    half = ROPE_DIM // 2
    s_hi = jnp.where((d >= half) & (d < ROPE_DIM), sin, 0.0)
    s_lo = jnp.where(d < half, -sin, 0.0)
    return cos, s_hi, s_lo


def _rope_tile(x, cos, s_hi, s_lo):
    half = ROPE_DIM // 2
    return x * cos + pltpu.roll(x, half, 1) * s_hi + pltpu.roll(x, LANES - half, 1) * s_lo


def _prompt_layer_kernel(x_ref, ng_ref, win_ref, cw_ref, cb_ref, lng_ref, lnb_ref, wco_ref,
                         sinks_ref, wao_ref, wout_ref, rope_ref, fng_ref,
                         y_ref, nconv_ref, nk_ref, nv_ref,
                         h_ref, full_ref, ga_ref, cc_ref, q_ref, kd_ref, vd_ref, gb_ref, og_ref,
                         mga_ref, mgb_ref, *, T, final_norm):
    i = pl.program_id(1)
    n_qblk = T // BLOCK

    @pl.when(i == 0)
    def _():
        full_ref[:, 0:CONV_PAD, :] = jnp.zeros((N_TILES, CONV_PAD, LANES), F32)
        kd_ref[:, 0:BLOCK, :] = jnp.zeros((N_KV_HEADS, BLOCK, LANES), BF16)
        vd_ref[:, :, 0:BLOCK, :] = jnp.zeros((N_KV_HEADS, 2, BLOCK, LANES), BF16)

    h_ref[...] = _rms_norm(x_ref[...], ng_ref[...]).astype(BF16)

    def in_proj(col, width=D_MODEL):
        return _dot(h_ref[...], win_ref[:, col:col + width])

    base = CONV_PAD - CONV_BUF
    half_rows = CONV_ROW_TILE // 2

    def conv_group(r0):
        acc = [[None] * N_TILES, [None] * N_TILES]
        for c in range(N_TILES):
            lanes = slice(c * LANES, (c + 1) * LANES)
            for j in range(CONV_KERNEL):
                w = cw_ref[j:j + 1, lanes]
                for par in range(2):
                    tap = full_ref[c, pl.ds(r0 + base + j + par, half_rows, stride=2), :] * w
                    acc[par][c] = tap if j == 0 else acc[par][c] + tap
        for par in range(2):
            rows = pl.ds(r0 + par, half_rows, stride=2)
            c_rows = jnp.concatenate(acc[par], axis=1) + cb_ref[...]
            y = _layer_norm(c_rows, lng_ref[...], lnb_ref[...])
            g_rows = jnp.concatenate([ga_ref[c, rows, :] for c in range(N_TILES)], axis=1)
            cc = _silu(y) * g_rows
            for c in range(N_TILES):
                cc_ref[c, rows, :] = cc[:, c * LANES:(c + 1) * LANES]

    cos, s_hi, s_lo = rope_ref[0], rope_ref[1], rope_ref[2]
    scale = HEAD_DIM ** -0.5 * LOG2E

    u = in_proj(COL_GLU_A) * _sigmoid(in_proj(COL_GLU_B))
    for c in range(N_TILES):
        full_ref[c, CONV_PAD:CONV_PAD + T, :] = u[:, c * LANES:(c + 1) * LANES]
    nconv_ref[...] = u[T - CONV_BUF:T, :]
    ga = _silu(in_proj(COL_GATE_A))
    for c in range(N_TILES):
        ga_ref[c] = ga[:, c * LANES:(c + 1) * LANES]

    q = in_proj(COL_Q)
    for m in range(N_TILES):
        lanes = slice(m * LANES, (m + 1) * LANES)
        q_ref[:, lanes] = (_rope_tile(q[:, lanes], cos, s_hi, s_lo) * scale).astype(BF16)
    k = _rope_tile(in_proj(COL_K, KV_WIDTH), cos, s_hi, s_lo)
    v = in_proj(COL_V, KV_WIDTH)
    nk_ref[...] = k[T - WINDOW:T, :]
    nv_ref[...] = v[T - WINDOW:T, :]
    lane = lax.broadcasted_iota(jnp.int32, (T, LANES), 1)
    lo = lane < HEAD_DIM
    k_sw = pltpu.roll(k, HEAD_DIM, 1)
    v_sw = pltpu.roll(v, HEAD_DIM, 1)
    kd_ref[0, BLOCK:BLOCK + T, :] = jnp.where(lo, k, k_sw).astype(BF16)
    kd_ref[1, BLOCK:BLOCK + T, :] = jnp.where(lo, k_sw, k).astype(BF16)
    vd_ref[0, 0, BLOCK:BLOCK + T, :] = jnp.where(lo, v, 0.0).astype(BF16)
    vd_ref[0, 1, BLOCK:BLOCK + T, :] = jnp.where(lo, 0.0, v_sw).astype(BF16)
    vd_ref[1, 0, BLOCK:BLOCK + T, :] = jnp.where(lo, v_sw, 0.0).astype(BF16)
    vd_ref[1, 1, BLOCK:BLOCK + T, :] = jnp.where(lo, 0.0, v).astype(BF16)
    for c0 in range(0, D_MODEL, PROJ_CHUNK):
        cols = slice(c0, c0 + PROJ_CHUNK)
        gb_ref[:, cols] = _silu(in_proj(COL_GATE_B + c0, PROJ_CHUNK))
        mga_ref[:, cols] = _sigmoid(in_proj(COL_MG_A + c0, PROJ_CHUNK))
        mgb_ref[:, cols] = _sigmoid(in_proj(COL_MG_B + c0, PROJ_CHUNK))

    qi = lax.broadcasted_iota(jnp.int32, (BLOCK, 2 * BLOCK), 0)
    kj = lax.broadcasted_iota(jnp.int32, (BLOCK, 2 * BLOCK), 1)
    band = (kj - qi >= 1) & (kj - qi <= WINDOW)
    first_key = jnp.where(i > 0, 0, BLOCK)
    band_first = band & (kj >= first_key)
    lane_b = lax.broadcasted_iota(jnp.int32, (BLOCK, LANES), 1)
    lo_b = lane_b < HEAD_DIM

    def attention_block(a):
        rows = slice(a * BLOCK, (a + 1) * BLOCK)
        keys = slice(a * BLOCK, a * BLOCK + 2 * BLOCK)
        mask = band_first if a == 0 else band
        scores = []
        for m in range(N_TILES):
            g = (2 * m) // GROUP
            q_tile = q_ref[rows, m * LANES:(m + 1) * LANES]
            for p in range(2):
                half = lo_b if p == 0 else jnp.logical_not(lo_b)
                qm = jnp.where(half, q_tile, jnp.zeros_like(q_tile))
                scores.append(_dot_nt(qm, kd_ref[g, keys, :]))
        probs, invs = [], []
        for head, s in enumerate(scores):
            sink = sinks_ref[head] * LOG2E
            s = jnp.where(mask, s, NEG)
            mx = jnp.maximum(jnp.max(s, axis=-1, keepdims=True), sink)
            pr = jnp.exp2(s - mx)
            den = jnp.sum(pr, axis=-1, keepdims=True) + jnp.exp2(sink - mx)
            probs.append(pr.astype(BF16))
            invs.append(1.0 / den)
        for m in range(N_TILES):
            g = (2 * m) // GROUP
            lanes = slice(m * LANES, (m + 1) * LANES)
            o_pair = (_dot(probs[2 * m], vd_ref[g, 0, keys, :])
                      + _dot(probs[2 * m + 1], vd_ref[g, 1, keys, :]))
            inv_pair = jnp.where(lo_b, invs[2 * m], invs[2 * m + 1])
            og_ref[rows, lanes] = (o_pair * inv_pair * gb_ref[rows, lanes]).astype(BF16)

    for a in range(n_qblk):
        attention_block(a)

    kd_ref[:, 0:BLOCK, :] = kd_ref[:, T:T + BLOCK, :]
    vd_ref[:, :, 0:BLOCK, :] = vd_ref[:, :, T:T + BLOCK, :]

    for r0 in range(0, T, CONV_ROW_TILE):
        conv_group(r0)
    full_ref[:, 0:CONV_PAD, :] = full_ref[:, T:T + CONV_PAD, :]

    cc_all = jnp.concatenate([cc_ref[c] for c in range(N_TILES)], axis=1).astype(BF16)
    y_a = _dot(cc_all, wco_ref[...])
    y_b = _dot(og_ref[...], wao_ref[...])
    y = (mga_ref[...] * y_a + mgb_ref[...] * y_b).astype(BF16)
    out = x_ref[...] + _dot(y, wout_ref[...])
    if final_norm:
        out = _rms_norm(out, fng_ref[...])
    y_ref[...] = out


def _full_spec(shape):
    nd = len(shape)
    return pl.BlockSpec(shape, lambda *_: (0,) * nd)


def _prompt_layer(x, ng, win, cw, cb, lng, lnb, wco, sinks, wao, wout, rope, fng,
                  *, layer, final_norm):
    B, S, _ = x.shape
    T = PROMPT_T
    kernel = functools.partial(_prompt_layer_kernel, T=T, final_norm=final_norm)
    row = (1, D_MODEL)

    def weight(*shape):
        nd = len(shape)
        return pl.BlockSpec((None,) + shape, lambda b, i: (layer,) + (0,) * nd,
                            pipeline_mode=pl.Buffered(1))

    in_specs = [
        pl.BlockSpec((None, T, D_MODEL), lambda b, i: (b, i, 0)),
        _full_spec(row),
        weight(D_MODEL, IN_COLS),
        _full_spec((CONV_KERNEL, D_MODEL)),
        _full_spec(row), _full_spec(row), _full_spec(row),
        weight(D_MODEL, D_MODEL),
        pl.BlockSpec(memory_space=pltpu.SMEM),
        weight(D_MODEL, D_MODEL),
        weight(D_MODEL, D_MODEL),
        pl.BlockSpec((3, T, LANES), lambda b, i: (0, i, 0)),
        _full_spec(row),
    ]
    out_specs = [
        pl.BlockSpec((None, T, D_MODEL), lambda b, i: (b, i, 0)),
        pl.BlockSpec((None, CONV_BUF, D_MODEL), lambda b, i: (b, 0, 0)),
        pl.BlockSpec((None, WINDOW, KV_WIDTH), lambda b, i: (b, 0, 0)),
        pl.BlockSpec((None, WINDOW, KV_WIDTH), lambda b, i: (b, 0, 0)),
    ]
    out_shape = [
        jax.ShapeDtypeStruct((B, S, D_MODEL), F32),
        jax.ShapeDtypeStruct((B, CONV_BUF, D_MODEL), F32),
        jax.ShapeDtypeStruct((B, WINDOW, KV_WIDTH), F32),
        jax.ShapeDtypeStruct((B, WINDOW, KV_WIDTH), F32),
    ]
    scratch = [
        pltpu.VMEM((T, D_MODEL), BF16),
        pltpu.VMEM((N_TILES, CONV_PAD + T, LANES), F32),
        pltpu.VMEM((N_TILES, T, LANES), F32),
        pltpu.VMEM((N_TILES, T, LANES), F32),
        pltpu.VMEM((T, D_MODEL), BF16),
        pltpu.VMEM((N_KV_HEADS, BLOCK + T, LANES), BF16),
        pltpu.VMEM((N_KV_HEADS, 2, BLOCK + T, LANES), BF16),
        pltpu.VMEM((T, D_MODEL), F32),
        pltpu.VMEM((T, D_MODEL), BF16),
        pltpu.VMEM((T, D_MODEL), F32),
        pltpu.VMEM((T, D_MODEL), F32),
    ]
    return pl.pallas_call(
        kernel,
        grid=(B, S // T),
        in_specs=in_specs,
        out_specs=out_specs,
        out_shape=out_shape,
        scratch_shapes=scratch,
        compiler_params=pltpu.CompilerParams(
            dimension_semantics=("arbitrary", "arbitrary"),
            vmem_limit_bytes=VMEM_LIMIT),
        name="prompt_layer",
    )(x, ng, win, cw, cb, lng, lnb, wco, sinks, wao, wout, rope, fng)


def _rope_table_kernel(invf_ref, out_ref, *, R):
    pos = (pl.program_id(0) * R + lax.broadcasted_iota(jnp.int32, (R, 1), 0)).astype(F32)
    for t, table in enumerate(_rope_tables(pos, invf_ref[...])):
        out_ref[t] = table


def _rope_table(invf, seq):
    R = ROPE_TABLE_ROWS
    return pl.pallas_call(
        functools.partial(_rope_table_kernel, R=R),
        grid=(seq // R,),
        in_specs=[_full_spec((1, LANES))],
        out_specs=pl.BlockSpec((3, R, LANES), lambda r: (0, r, 0)),
        out_shape=jax.ShapeDtypeStruct((3, seq, LANES), F32),
        name="rope_table",
    )(invf)


def _decode_kernel(x_ref, st_ref, ck_ref, cv_ref, ng_ref, win_ref, cw_ref, cb_ref, lng_ref,
                   lnb_ref, wco_ref, sinks_ref, wao_ref, wout_ref, invf_ref, fng_ref,
                   y_ref, nconv_ref, nk_ref, nv_ref,
                   xcur_ref, h_ref, u_ref, q_ref, kt_ref, vt_ref, c_ref, o_ref, qt_ref, ot_ref,
                   oblk_ref, *, NS):
    l = pl.program_id(0)
    s = pl.program_id(1)
    last_layer = pl.num_programs(0) - 1
    last = pl.num_programs(1) - 1
    N = x_ref.shape[0]

    @pl.when(s == 0)
    def _():
        @pl.when(l == 0)
        def _():
            xcur_ref[...] = x_ref[...]

        h = _rms_norm(xcur_ref[...], ng_ref[...]).astype(BF16)
        h_ref[...] = h
        za = _dot(h, win_ref[:, COL_GLU_A:COL_GLU_A + D_MODEL])
        zb = _dot(h, win_ref[:, COL_GLU_B:COL_GLU_B + D_MODEL])
        u_ref[...] = za * _sigmoid(zb)
        pos = jnp.full((N, 1), float(PAST_LEN), F32)
        cos, s_hi, s_lo = _rope_tables(pos, invf_ref[...])
        scale = HEAD_DIM ** -0.5
        q = _dot(h, win_ref[:, COL_Q:COL_Q + D_MODEL])
        for m in range(N_TILES):
            lanes = slice(m * LANES, (m + 1) * LANES)
            q_ref[:, lanes] = _rope_tile(q[:, lanes], cos, s_hi, s_lo) * scale
        k = _rope_tile(_dot(h, win_ref[:, COL_K:COL_K + KV_WIDTH]), cos, s_hi, s_lo)
        v = _dot(h, win_ref[:, COL_V:COL_V + KV_WIDTH])
        kt_ref[...] = k.T
        vt_ref[...] = v.T

    sink_col = sinks_ref[...]
    row8 = lax.broadcasted_iota(jnp.int32, (N_TILES, LANES), 0)
    lane8 = lax.broadcasted_iota(jnp.int32, (N_TILES, LANES), 1)
    lo8 = lane8 < HEAD_DIM
    g0 = row8 < (N_TILES // 2)
    lane_w = lax.broadcasted_iota(jnp.int32, (KV_WIDTH, WINDOW), 1)
    newest = lane_w == WINDOW - 1

    blk = pl.ds(pl.multiple_of(s * NS, NS), NS)
    u_blk = u_ref[blk, :]
    q_blk = q_ref[blk, :]

    acc = u_blk * cw_ref[CONV_BUF:CONV_KERNEL, :]
    for j in range(CONV_BUF):
        acc = acc + st_ref[j] * cw_ref[j:j + 1, :]
    c_ref[blk, :] = acc + cb_ref[...]
    for j in range(CONV_BUF - 1):
        nconv_ref[j] = st_ref[j + 1]
    nconv_ref[CONV_BUF - 1] = u_blk

    samples = range(NS)
    mine = [lane_w == s * NS + n for n in samples]
    k_col = [jnp.sum(jnp.where(mine[n], kt_ref[...], 0.0), axis=1, keepdims=True) for n in samples]
    v_col = [jnp.sum(jnp.where(mine[n], vt_ref[...], 0.0), axis=1, keepdims=True) for n in samples]
    kt = [jnp.where(newest, k_col[n], pltpu.roll(ck_ref[n], WINDOW - 1, 1)) for n in samples]
    vt = [jnp.where(newest, v_col[n], pltpu.roll(cv_ref[n], WINDOW - 1, 1)) for n in samples]
    for n in samples:
        nk_ref[n] = kt[n]
        nv_ref[n] = vt[n]
    for n in samples:
        for m in range(N_TILES):
            qt_ref[n, m:m + 1, :] = q_blk[n:n + 1, m * LANES:(m + 1) * LANES]
    lhs = []
    for n in samples:
        qt = qt_ref[n]
        qe = jnp.where(lo8, qt, 0.0)
        qo = jnp.where(lo8, 0.0, qt)
        q_even = jnp.where(g0, qe, pltpu.roll(qe, HEAD_DIM, 1))
        q_odd = jnp.where(g0, pltpu.roll(qo, HEAD_DIM, 1), qo)
        lhs.append(jnp.concatenate([q_even, q_odd], axis=0).astype(BF16))
    sc = [_dot(lhs[n], kt[n].astype(BF16)) for n in samples]
    pr, inv = [], []
    for n in samples:
        mx = jnp.maximum(jnp.max(sc[n], axis=-1, keepdims=True), sink_col)
        p = jnp.exp(sc[n] - mx)
        inv.append(1.0 / (jnp.sum(p, axis=-1, keepdims=True) + jnp.exp(sink_col - mx)))
        pr.append(p.astype(BF16))
    o = [_dot_nt(pr[n], vt[n].astype(BF16)) * inv[n] for n in samples]
    for n in samples:
        o_even = o[n][0:N_TILES, :]
        o_odd = o[n][N_TILES:N_HEADS, :]
        oe = jnp.where(g0, o_even, pltpu.roll(o_even, HEAD_DIM, 1))
        oo = jnp.where(g0, pltpu.roll(o_odd, HEAD_DIM, 1), o_odd)
        ot_ref[n] = jnp.where(lo8, oe, oo)
    for n in samples:
        for m in range(N_TILES):
            oblk_ref[n:n + 1, m * LANES:(m + 1) * LANES] = ot_ref[n, m:m + 1, :]
    o_ref[blk, :] = oblk_ref[...]

    @pl.when(s == last)
    def _():
        h = h_ref[...]
        ga = _silu(_dot(h, win_ref[:, COL_GATE_A:COL_GATE_A + D_MODEL]))
        cc = _silu(_layer_norm(c_ref[...], lng_ref[...], lnb_ref[...])) * ga
        y_a = _dot(cc.astype(BF16), wco_ref[...])
        gb = _silu(_dot(h, win_ref[:, COL_GATE_B:COL_GATE_B + D_MODEL]))
        y_b = _dot((o_ref[...] * gb).astype(BF16), wao_ref[...])
        mg_a = _sigmoid(_dot(h, win_ref[:, COL_MG_A:COL_MG_A + D_MODEL]))
        mg_b = _sigmoid(_dot(h, win_ref[:, COL_MG_B:COL_MG_B + D_MODEL]))
        y = (mg_a * y_a + mg_b * y_b).astype(BF16)
        out = xcur_ref[...] + _dot(y, wout_ref[...])
        xcur_ref[...] = out

        @pl.when(l == last_layer)
        def _():
            y_ref[...] = _rms_norm(out, fng_ref[...])


def _decode(x, st, ck, cv, ng, win, cw, cb, lng, lnb, wco, sinks, wao, wout, invf, fng):
    N = x.shape[0]
    depth = win.shape[0]
    NS = SAMPLE_NS

    def per_layer(*shape):
        nd = len(shape)
        return pl.BlockSpec((None,) + shape, lambda l, s: (l,) + (0,) * nd)

    def weight(*shape):
        nd = len(shape)
        return pl.BlockSpec((None,) + shape, lambda l, s: (l,) + (0,) * nd,
                            pipeline_mode=pl.Buffered(1))

    in_specs = [
        pl.BlockSpec((N, D_MODEL), lambda l, s: (0, 0)),
        pl.BlockSpec((None, CONV_BUF, NS, D_MODEL), lambda l, s: (l, 0, s, 0)),
        pl.BlockSpec((None, NS, KV_WIDTH, WINDOW), lambda l, s: (l, s, 0, 0)),
        pl.BlockSpec((None, NS, KV_WIDTH, WINDOW), lambda l, s: (l, s, 0, 0)),
        per_layer(1, D_MODEL),
        weight(D_MODEL, IN_COLS),
        per_layer(CONV_KERNEL, D_MODEL),
        per_layer(1, D_MODEL), per_layer(1, D_MODEL), per_layer(1, D_MODEL),
        weight(D_MODEL, D_MODEL),
        per_layer(N_HEADS, 1),
        weight(D_MODEL, D_MODEL),
        weight(D_MODEL, D_MODEL),
        pl.BlockSpec((1, LANES), lambda l, s: (0, 0)),
        pl.BlockSpec((1, D_MODEL), lambda l, s: (0, 0)),
    ]
    out_specs = [
        pl.BlockSpec((N, D_MODEL), lambda l, s: (0, 0)),
        pl.BlockSpec((None, CONV_BUF, NS, D_MODEL), lambda l, s: (l, 0, s, 0)),
        pl.BlockSpec((None, NS, KV_WIDTH, WINDOW), lambda l, s: (l, s, 0, 0)),
        pl.BlockSpec((None, NS, KV_WIDTH, WINDOW), lambda l, s: (l, s, 0, 0)),
    ]
    out_shape = [
        jax.ShapeDtypeStruct((N, D_MODEL), F32),
        jax.ShapeDtypeStruct((depth, CONV_BUF, N, D_MODEL), F32),
        jax.ShapeDtypeStruct((depth, N, KV_WIDTH, WINDOW), F32),
        jax.ShapeDtypeStruct((depth, N, KV_WIDTH, WINDOW), F32),
    ]
    scratch = [
        pltpu.VMEM((N, D_MODEL), F32),
        pltpu.VMEM((N, D_MODEL), BF16),
        pltpu.VMEM((N, D_MODEL), F32),
        pltpu.VMEM((N, D_MODEL), F32),
        pltpu.VMEM((KV_WIDTH, N), F32),
        pltpu.VMEM((KV_WIDTH, N), F32),
        pltpu.VMEM((N, D_MODEL), F32),
        pltpu.VMEM((N, D_MODEL), F32),
        pltpu.VMEM((NS, N_TILES, LANES), F32),
        pltpu.VMEM((NS, N_TILES, LANES), F32),
        pltpu.VMEM((NS, D_MODEL), F32),
    ]
    return pl.pallas_call(
        functools.partial(_decode_kernel, NS=NS),
        grid=(depth, N // NS),
        in_specs=in_specs,
        out_specs=out_specs,
        out_shape=out_shape,
        scratch_shapes=scratch,
        compiler_params=pltpu.CompilerParams(
            dimension_semantics=("arbitrary", "arbitrary"),
            vmem_limit_bytes=VMEM_LIMIT),
        name="decode_layers",
    )(x, st, ck, cv, ng, win, cw, cb, lng, lnb, wco, sinks, wao, wout, invf, fng)


def _inv_freq_lanes():
    inv = ROPE_THETA ** (-jnp.arange(0, ROPE_DIM, 2, dtype=F32) / ROPE_DIM)
    d = jnp.arange(LANES) % HEAD_DIM
    return jnp.where(d < ROPE_DIM, inv[d % (ROPE_DIM // 2)], 0.0).astype(F32)[None, :]


def kernel(x_prompt, x_sample, state_conv, cache_k_win, cache_v_win, norm_g, w_in, conv_w, conv_b,
           conv_ln_g, conv_ln_b, w_conv_out, attn_sinks, w_attn_out, w_out, final_norm_g):
    depth = w_in.shape[0]
    n_batch, seq = x_prompt.shape[0], x_prompt.shape[1]
    n_dec = x_sample.shape[0]
    invf = _inv_freq_lanes()
    fng = final_norm_g[None, :]
    win, wco, wao, wout = (w.astype(BF16) for w in (w_in, w_conv_out, w_attn_out, w_out))
    row = lambda p: p[:, None, :]

    st = jnp.transpose(state_conv, (0, 2, 1, 3))
    ck = jnp.transpose(cache_k_win, (0, 1, 3, 4, 2)).reshape(depth, n_dec, KV_WIDTH, WINDOW)
    cv = jnp.transpose(cache_v_win, (0, 1, 3, 4, 2)).reshape(depth, n_dec, KV_WIDTH, WINDOW)
    sinks_col = jnp.swapaxes(attn_sinks.reshape(depth, N_TILES, 2), 1, 2).reshape(depth, N_HEADS, 1)
    ys, nconv_s, nk_s, nv_s = _decode(
        x_sample[:, 0, :], st, ck, cv, row(norm_g), win, conv_w, row(conv_b), row(conv_ln_g),
        row(conv_ln_b), wco, sinks_col, wao, wout, invf, fng)
    kv_t = (depth, n_dec, N_KV_HEADS, HEAD_DIM, WINDOW)
    new_conv_sample = jnp.transpose(nconv_s, (0, 2, 1, 3))
    new_k_sample = jnp.transpose(nk_s.reshape(kv_t), (0, 1, 4, 2, 3))
    new_v_sample = jnp.transpose(nv_s.reshape(kv_t), (0, 1, 4, 2, 3))

    rope = _rope_table(invf, seq)
    hp = x_prompt
    outs_p = []
    for l in range(depth):
        hp, c1, k1, v1 = _prompt_layer(
            hp, norm_g[l][None, :], win, conv_w[l], conv_b[l][None, :], conv_ln_g[l][None, :],
            conv_ln_b[l][None, :], wco, attn_sinks[l], wao, wout, rope, fng,
            layer=l, final_norm=l == depth - 1)
        outs_p.append((c1, k1, v1))
    kv_p = (depth, n_batch, WINDOW, N_KV_HEADS, HEAD_DIM)
    return (hp, ys[:, None, :],
            jnp.stack([o[0] for o in outs_p]),
            jnp.stack([o[1] for o in outs_p]).reshape(kv_p),
            jnp.stack([o[2] for o in outs_p]).reshape(kv_p),
            new_conv_sample, new_k_sample, new_v_sample)
```
